```python
import jax, jax.numpy as jnp
from jax import lax
import numpy as np

D_MODEL = 1024
BATCH = 4
SEQ = 8192
DEPTH = 1
DEC_BATCH = 8
DEC_SEQ = 4096
PAST_LEN = 128

D_MIX = D_MODEL
D_REC = D_MIX // 2
D_FFT = D_MIX - D_REC
N_REC_HEADS = 8
REC_HEAD_DIM = D_REC // N_REC_HEADS
N_FFT_GROUPS = 4
FFT_GROUP_DIM = D_FFT // N_FFT_GROUPS
D_IN = 2 * D_REC + D_FFT
CONV_WIDTH = 4
CONV_LEFT = 2
LRU_C = 8.0
N_EXPERTS = 32
TOP_K = 4
D_FF = D_MODEL
SWIGLU_LIMIT = 7.0
SWIGLU_ALPHA = 1.702
MOE_BLOCK = 128
DEEPNORM_ALPHA = (2.0 * DEPTH) ** 0.25
DEEPNORM_BETA = (8.0 * DEPTH) ** -0.25
LN_EPS = 1e-5

kernel_name = 'hymba_rglru_fnet_moe_encoder'


def _layer_norm(x, g=None, b=None):
    xf = x.astype(jnp.float32)
    mu = jnp.mean(xf, -1, keepdims=True)
    var = jnp.mean(jnp.square(xf - mu), -1, keepdims=True)
    y = (xf - mu) * lax.rsqrt(var + LN_EPS)
    if g is not None:
        y = y * g.astype(jnp.float32) + b.astype(jnp.float32)
    return y.astype(x.dtype)


def _rms_norm(x, g):
    xf = x.astype(jnp.float32)
    y = xf * lax.rsqrt(jnp.mean(jnp.square(xf), -1, keepdims=True) + LN_EPS)
    return y * g.astype(jnp.float32)


def _centred_dwconv(x, w, b):
    y = lax.conv_general_dilated(
        x, w[:, None, :].astype(x.dtype), window_strides=(1,),
        padding=[(CONV_LEFT, CONV_WIDTH - 1 - CONV_LEFT)],
        dimension_numbers=('NWC', 'WIO', 'NWC'), feature_group_count=x.shape[-1])
    return y + b.astype(x.dtype)


def _linear_scan(a, b, reverse):
    def combine(left, right):
        a_l, b_l = left
        a_r, b_r = right
        return a_l * a_r, a_r * b_l + b_r
    _, h = lax.associative_scan(combine, (a, b), axis=1, reverse=reverse)
    return h


def _rg_lru(xc, w_gate, b_gate, lam, reverse):
    bsz, slen, _ = xc.shape
    xh = xc.reshape(bsz, slen, N_REC_HEADS, REC_HEAD_DIM)
    gates = jnp.einsum('bshi,ghij->gbshj', xh, w_gate.astype(jnp.float32)) + b_gate.astype(jnp.float32)[:, None, None]
    r = jax.nn.sigmoid(gates[0]).reshape(bsz, slen, D_REC)
    i = jax.nn.sigmoid(gates[1]).reshape(bsz, slen, D_REC)
    log_a = -LRU_C * r * jax.nn.softplus(-lam.astype(jnp.float32))
    a = jnp.exp(log_a)
    mult = jnp.sqrt(-jnp.expm1(2.0 * log_a))
    return _linear_scan(a, mult * (i * xc), reverse)


def _mixer(h, w_in, conv_w, conv_b, w_rg, b_rg, lam, g_mix, w_out):
    u = h @ w_in
    u_x = u[..., :D_REC]
    u_g = u[..., D_REC:2 * D_REC]
    u_f = u[..., 2 * D_REC:]
    xc = _centred_dwconv(u_x, conv_w, conv_b).astype(jnp.float32)
    rec = _rg_lru(xc, w_rg[0], b_rg[0], lam[0], False) + _rg_lru(xc, w_rg[1], b_rg[1], lam[1], True)
    rec = rec * jax.nn.gelu(u_g.astype(jnp.float32))
    bsz, slen, _ = u_f.shape
    uf = u_f.astype(jnp.float32).reshape(bsz, slen, N_FFT_GROUPS, FFT_GROUP_DIM)
    four = jnp.fft.fftn(uf, axes=(1, 3), norm='ortho').real.reshape(bsz, slen, D_FFT)
    y = jnp.concatenate([_rms_norm(rec, g_mix[:D_REC]), _rms_norm(four, g_mix[D_REC:])], axis=-1)
    return y.astype(h.dtype) @ w_out


def _moe(h, w_router, b_router, w_gate_up, b_gate_up, w_down, b_down):
    bsz, slen, d = h.shape
    xt = h.reshape(-1, d)
    n_tok = xt.shape[0]
    logits = (xt @ w_router + b_router).astype(jnp.float32)
    top_v, top_e = lax.top_k(logits, TOP_K)
    top_w = jax.nn.softmax(top_v, axis=-1)
    n_asg = n_tok * TOP_K
    flat_e = top_e.reshape(-1)
    flat_tok = jnp.repeat(jnp.arange(n_tok, dtype=jnp.int32), TOP_K)
    flat_w = top_w.reshape(-1)
    order = jnp.argsort(flat_e)
    sorted_e = flat_e[order]
    counts = jnp.bincount(flat_e, length=N_EXPERTS)
    starts = jnp.cumsum(counts) - counts
    padded = (counts + MOE_BLOCK - 1) // MOE_BLOCK * MOE_BLOCK
    pad_ends = jnp.cumsum(padded)
    pad_starts = pad_ends - padded
    dest = pad_starts[sorted_e] + jnp.arange(n_asg, dtype=jnp.int32) - starts[sorted_e]
    n_blocks = -(-n_asg // MOE_BLOCK) + N_EXPERTS
    cap = n_blocks * MOE_BLOCK
    slot_tok = jnp.zeros((cap,), jnp.int32).at[dest].set(flat_tok[order])
    slot_w = jnp.zeros((cap,), jnp.float32).at[dest].set(flat_w[order])
    blk_e = jnp.minimum(jnp.searchsorted(pad_ends, jnp.arange(n_blocks, dtype=jnp.int32) * MOE_BLOCK, side='right'), N_EXPERTS - 1)

    def expert_block(args):
        e, tok, w = args
        xb = xt[tok]
        gu = xb @ w_gate_up[e] + b_gate_up[e]
        gate = jnp.minimum(gu[:, :D_FF], SWIGLU_LIMIT)
        up = jnp.clip(gu[:, D_FF:], -SWIGLU_LIMIT, SWIGLU_LIMIT)
        act = (up + 1) * (gate * jax.nn.sigmoid(SWIGLU_ALPHA * gate))
        out = act @ w_down[e] + b_down[e]
        return out.astype(jnp.float32) * w[:, None]

    outs = lax.map(expert_block, (blk_e, slot_tok.reshape(n_blocks, MOE_BLOCK), slot_w.reshape(n_blocks, MOE_BLOCK)))
    y = jnp.zeros((n_tok, d), jnp.float32).at[slot_tok].add(outs.reshape(cap, d))
    return y.astype(h.dtype).reshape(bsz, slen, d)


def _trunk(x, c, w_ada, b_ada, w_in, conv_w, conv_b, w_rg, b_rg, lru_lambda, g_mix, w_out,
           ln1_g, ln1_b, w_router, b_router, w_gate_up, b_gate_up, w_down, b_down, ln2_g, ln2_b):
    for l in range(DEPTH):
        mod = (jax.nn.silu(c) @ w_ada[l] + b_ada[l])[:, None, :]
        shift1, scale1, gate1, shift2, scale2, gate2 = jnp.split(mod, 6, axis=-1)
        h = _layer_norm(x) * (1 + scale1) + shift1
        mix = _mixer(h, w_in[l], conv_w[l], conv_b[l], w_rg[l], b_rg[l], lru_lambda[l], g_mix[l], w_out[l])
        x = _layer_norm(DEEPNORM_ALPHA * x + gate1 * mix, ln1_g[l], ln1_b[l])
        h = _layer_norm(x) * (1 + scale2) + shift2
        ffn = _moe(h, w_router[l], b_router[l], w_gate_up[l], b_gate_up[l], w_down[l], b_down[l])
        x = _layer_norm(DEEPNORM_ALPHA * x + gate2 * ffn, ln2_g[l], ln2_b[l])
    return x


def setup_inputs(seed: int = 0) -> dict:
    key = jax.random.key(seed)
    ks = jax.random.split(key, 26)
    f32 = jnp.float32

    def nrm(k, shape, s):
        return jax.random.normal(k, shape, f32) * s

    u = jax.random.uniform(ks[9], (DEPTH, 2, D_REC), f32, 0.9, 0.999)
    sig = u ** (1.0 / LRU_C)
    lru_lambda = jnp.log(sig) - jnp.log1p(-sig)
    return {
        'x_prompt': nrm(ks[0], (BATCH, SEQ, D_MODEL), 1.0),
        'x_sample': nrm(ks[1], (DEC_BATCH, DEC_SEQ, D_MODEL), 1.0),
        'c_prompt': nrm(ks[2], (BATCH, D_MODEL), 1.0),
        'c_sample': nrm(ks[3], (DEC_BATCH, D_MODEL), 1.0),
        'w_ada': nrm(ks[4], (DEPTH, D_MODEL, 6 * D_MODEL), 0.5 * D_MODEL ** -0.5),
        'b_ada': nrm(ks[5], (DEPTH, 6 * D_MODEL), 0.01),
        'w_in': nrm(ks[6], (DEPTH, D_MODEL, D_IN), D_MODEL ** -0.5),
        'conv_w': nrm(ks[7], (DEPTH, CONV_WIDTH, D_REC), CONV_WIDTH ** -0.5),
        'conv_b': nrm(ks[8], (DEPTH, D_REC), 0.01),
        'w_rg': nrm(ks[10], (DEPTH, 2, 2, N_REC_HEADS, REC_HEAD_DIM, REC_HEAD_DIM), REC_HEAD_DIM ** -0.5),
        'b_rg': nrm(ks[11], (DEPTH, 2, 2, N_REC_HEADS, REC_HEAD_DIM), 0.01),
        'lru_lambda': lru_lambda,
        'g_mix': 1.0 + nrm(ks[12], (DEPTH, D_MIX), 0.01),
        'w_out': nrm(ks[13], (DEPTH, D_MIX, D_MODEL), DEEPNORM_BETA * D_MIX ** -0.5),
        'ln1_g': 1.0 + nrm(ks[14], (DEPTH, D_MODEL), 0.01),
        'ln1_b': nrm(ks[15], (DEPTH, D_MODEL), 0.01),
        'w_router': nrm(ks[16], (DEPTH, D_MODEL, N_EXPERTS), D_MODEL ** -0.5),
        'b_router': nrm(ks[17], (DEPTH, N_EXPERTS), 0.01),
        'w_gate_up': nrm(ks[18], (DEPTH, N_EXPERTS, D_MODEL, 2 * D_FF), D_MODEL ** -0.5),
        'b_gate_up': nrm(ks[19], (DEPTH, N_EXPERTS, 2 * D_FF), 0.01),
        'w_down': nrm(ks[20], (DEPTH, N_EXPERTS, D_FF, D_MODEL), DEEPNORM_BETA * D_FF ** -0.5),
        'b_down': nrm(ks[21], (DEPTH, N_EXPERTS, D_MODEL), 0.01),
        'ln2_g': 1.0 + nrm(ks[22], (DEPTH, D_MODEL), 0.01),
        'ln2_b': nrm(ks[23], (DEPTH, D_MODEL), 0.01),
    }


def reference(x_prompt, x_sample, c_prompt, c_sample, w_ada, b_ada, w_in, conv_w, conv_b, w_rg, b_rg,
              lru_lambda, g_mix, w_out, ln1_g, ln1_b, w_router, b_router, w_gate_up, b_gate_up,
              w_down, b_down, ln2_g, ln2_b):
    params = (w_ada, b_ada, w_in, conv_w, conv_b, w_rg, b_rg, lru_lambda, g_mix, w_out,
              ln1_g, ln1_b, w_router, b_router, w_gate_up, b_gate_up, w_down, b_down, ln2_g, ln2_b)
    y_prompt = _trunk(x_prompt, c_prompt, *params)
    y_sample = _trunk(x_sample, c_sample, *params)
    return (y_prompt, y_sample)
```

```python
import functools
import math

import numpy as np
import jax
import jax.numpy as jnp
from jax import lax
from jax.experimental import pallas as pl
from jax.experimental.pallas import tpu as pltpu

D_MODEL = 1024
D_REC = 512
D_FFT = 512
N_REC_HEADS = 8
REC_HEAD_DIM = 64
FFT_GROUP_DIM = 128
N_FFT_GROUPS = 4
CONV_WIDTH = 4
CONV_LEFT = 2
LRU_C = 8.0
N_EXPERTS = 32
TOP_K = 4
D_FF = 1024
SWIGLU_LIMIT = 7.0
SWIGLU_ALPHA = 1.702
DEEPNORM_ALPHA = 2.0 ** 0.25
LN_EPS = 1e-5

LANES = 128
SUBLANES = 8
DFT_N2 = 128
VMEM_LIMIT = 48 * 1024 * 1024

F32 = jnp.float32
BF16 = jnp.bfloat16


def _params(sem, vmem=VMEM_LIMIT):
    return pltpu.CompilerParams(dimension_semantics=sem, vmem_limit_bytes=vmem)


def _dot(a, b):
    return jnp.dot(a, b, preferred_element_type=F32)


def _dot3(a, b):
    a_hi = a.astype(BF16)
    b_hi = b.astype(BF16)
    a_lo = (a - a_hi.astype(F32)).astype(BF16)
    b_lo = (b - b_hi.astype(F32)).astype(BF16)
    return _dot(a_hi, b_hi) + _dot(a_lo, b_hi) + _dot(a_hi, b_lo)


def _ln(x):
    mu = jnp.mean(x, axis=-1, keepdims=True)
    xc = x - mu
    var = jnp.mean(xc * xc, axis=-1, keepdims=True)
    return xc * lax.rsqrt(var + LN_EPS)


def _ada_kernel(c_ref, w_ref, b_ref, o_ref):
    c = c_ref[...]
    s = c * jax.nn.sigmoid(c)
    o_ref[...] = _dot3(s, w_ref[...]) + b_ref[...]


def _ada_mod(c_all, w_ada, b_ada):
    bp = c_all.shape[0]
    n = w_ada.shape[1]
    tn = 768
    return pl.pallas_call(
        _ada_kernel,
        out_shape=jax.ShapeDtypeStruct((bp, n), F32),
        grid=(n // tn,),
        in_specs=[pl.BlockSpec((bp, D_MODEL), lambda j: (0, 0)),
                  pl.BlockSpec((D_MODEL, tn), lambda j: (0, j)),
                  pl.BlockSpec((1, tn), lambda j: (0, j))],
        out_specs=pl.BlockSpec((bp, tn), lambda j: (0, j)),
        compiler_params=_params(("arbitrary",)),
        name="ada_mod",
    )(c_all, w_ada, b_ada.reshape(1, n))


def _inproj_kernel(x_ref, mod_ref, w_ref, dft_ref, ux_ref, ug_ref, zr_ref, zi_ref):
    m = mod_ref[0]
    h = _ln(x_ref[...]) * (1.0 + m[1:2]) + m[0:1]
    u = _dot(h.astype(BF16), w_ref[...])
    ux_ref[...] = u[:, :D_REC]
    ug_ref[...] = u[:, D_REC:2 * D_REC]
    for g in range(N_FFT_GROUPS):
        lo = 2 * D_REC + g * FFT_GROUP_DIM
        z = _dot(u[:, lo:lo + FFT_GROUP_DIM].astype(BF16), dft_ref[...])
        zr_ref[:, g * FFT_GROUP_DIM:(g + 1) * FFT_GROUP_DIM] = z[:, :FFT_GROUP_DIM]
        zi_ref[:, g * FFT_GROUP_DIM:(g + 1) * FFT_GROUP_DIM] = z[:, FFT_GROUP_DIM:]


def _group_dft_matrix():
    j = np.arange(FFT_GROUP_DIM)
    ang = 2.0 * np.pi * ((j[:, None] * j[None, :]) % FFT_GROUP_DIM) / FFT_GROUP_DIM
    return jnp.asarray(np.concatenate([np.cos(ang), -np.sin(ang)], axis=1), dtype=BF16)


def _inproj(x2d, mod3, w_in_bf, seq, b0):
    t = x2d.shape[0]
    tm = 512
    row = pl.BlockSpec((tm, D_REC), lambda i: (i, 0))
    return pl.pallas_call(
        _inproj_kernel,
        out_shape=[jax.ShapeDtypeStruct((t, D_REC), F32)] * 4,
        grid=(t // tm,),
        in_specs=[pl.BlockSpec((tm, D_MODEL), lambda i: (i, 0)),
                  pl.BlockSpec((1, 6, D_MODEL), lambda i: (b0 + (i * tm) // seq, 0, 0)),
                  pl.BlockSpec((D_MODEL, 3 * D_REC), lambda i: (0, 0)),
                  pl.BlockSpec((FFT_GROUP_DIM, 2 * FFT_GROUP_DIM), lambda i: (0, 0))],
        out_specs=[row, row, row, row],
        compiler_params=_params(("arbitrary",)),
        name="inproj",
    )(x2d, mod3, w_in_bf, _group_dft_matrix())


REC_CHUNK = 256


def _scan_chunk(a, b, carry, h_ref, reverse):
    rows = a.shape[0]
    ridx = lax.broadcasted_iota(jnp.int32, a.shape, 0) % SUBLANES
    for s in (1, 2, 4):
        if reverse:
            a_sh = pltpu.roll(a, rows - s, 0)
            b_sh = pltpu.roll(b, rows - s, 0)
            keep = ridx < SUBLANES - s
        else:
            a_sh = pltpu.roll(a, s, 0)
            b_sh = pltpu.roll(b, s, 0)
            keep = ridx >= s
        b = jnp.where(keep, a * b_sh + b, b)
        a = jnp.where(keep, a * a_sh, a)
    n_tiles = rows // SUBLANES
    order = range(n_tiles - 1, -1, -1) if reverse else range(n_tiles)
    for j in order:
        sl = slice(j * SUBLANES, (j + 1) * SUBLANES)
        h = b[sl] + a[sl] * carry
        carry = h[0:1] if reverse else h[SUBLANES - 1:SUBLANES]
        h_ref[sl, :] = h
    return carry


def _conv_chunk(prev_ref, x_ref, next_ref, cw_ref, cb_ref, first, last):
    rows = x_ref.shape[0]
    prev = jnp.where(first, 0.0, prev_ref[...])
    nxt = jnp.where(last, 0.0, next_ref[...])
    ext = jnp.concatenate([prev, x_ref[...], nxt], axis=0)
    n = rows + 2 * SUBLANES
    body = slice(SUBLANES, SUBLANES + rows)
    cw = cw_ref[...]
    xc = cw[2:3] * ext[body]
    xc = xc + cw[0:1] * pltpu.roll(ext, 2, 0)[body]
    xc = xc + cw[1:2] * pltpu.roll(ext, 1, 0)[body]
    xc = xc + cw[3:4] * pltpu.roll(ext, n - 1, 0)[body]
    return xc + cb_ref[...]


def _neg_expm1(y, exp_y):
    series = y * (1.0 / 5040.0)
    for c in (1.0 / 720.0, 1.0 / 120.0, 1.0 / 24.0, 1.0 / 6.0, 0.5, 1.0):
        series = y * (series + c)
    return jnp.where(y > -0.25, -series, 1.0 - exp_y)


def _lru_coeffs(xc, wg_ref, bg_ref, lam_ref):
    lam = lam_ref[...]
    neg = -lam
    softplus = jnp.maximum(neg, 0.0) + jnp.log(1.0 + jnp.exp(-jnp.abs(neg)))
    xb = xc.astype(BF16)
    r_parts, i_parts = [], []
    for cb in range(D_REC // LANES):
        g = _dot(xb[:, cb * LANES:(cb + 1) * LANES], wg_ref[cb])
        r_parts.append(g[:, :LANES])
        i_parts.append(g[:, LANES:])
    bg = bg_ref[...]
    r = jax.nn.sigmoid(jnp.concatenate(r_parts, axis=1) + bg[0:1])
    i = jax.nn.sigmoid(jnp.concatenate(i_parts, axis=1) + bg[1:2])
    log_a = (-LRU_C) * r * softplus
    a = jnp.exp(log_a)
    mult = jnp.sqrt(_neg_expm1(2.0 * log_a, a * a))
    return a, mult * (i * xc)


def _rec_fwd_kernel(prev_ref, x_ref, next_ref, cw_ref, cb_ref, wg_ref, bg_ref, lam_ref,
                    hf_ref, carry_ref):
    c = pl.program_id(1)
    nch = pl.num_programs(1)

    @pl.when(c == 0)
    def _():
        carry_ref[...] = jnp.zeros_like(carry_ref)

    xc = _conv_chunk(prev_ref, x_ref, next_ref, cw_ref, cb_ref, c == 0, c == nch - 1)
    a, b = _lru_coeffs(xc, wg_ref, bg_ref, lam_ref)
    carry_ref[...] = _scan_chunk(a, b, carry_ref[...], hf_ref, reverse=False)


def _rec_bwd_kernel(prev_ref, x_ref, next_ref, cw_ref, cb_ref, wg_ref, bg_ref, lam_ref,
                    hf_ref, ug_ref, rec_ref, carry_ref, hb_ref):
    j = pl.program_id(1)
    nch = pl.num_programs(1)

    @pl.when(j == 0)
    def _():
        carry_ref[...] = jnp.zeros_like(carry_ref)

    xc = _conv_chunk(prev_ref, x_ref, next_ref, cw_ref, cb_ref, j == nch - 1, j == 0)
    a, b = _lru_coeffs(xc, wg_ref, bg_ref, lam_ref)
    carry_ref[...] = _scan_chunk(a, b, carry_ref[...], hb_ref, reverse=True)
    ug = ug_ref[...]
    gelu = 0.5 * ug * (1.0 + jnp.tanh(math.sqrt(2.0 / math.pi) * (ug + 0.044715 * (ug * ug * ug))))
    rec_ref[...] = (hf_ref[...] + hb_ref[...]) * gelu


def _rec_specs(batch, seq, reverse):
    lc = REC_CHUNK
    nch = seq // lc
    tiles = lc // SUBLANES
    last_tile = batch * nch * tiles - 1

    def chunk(b, j):
        return nch - 1 - j if reverse else j

    row = pl.BlockSpec((lc, D_REC), lambda b, j: (b * nch + chunk(b, j), 0))
    prev = pl.BlockSpec((SUBLANES, D_REC),
                        lambda b, j: (jnp.maximum((b * nch + chunk(b, j)) * tiles - 1, 0), 0))
    nxt = pl.BlockSpec((SUBLANES, D_REC),
                       lambda b, j: (jnp.minimum((b * nch + chunk(b, j) + 1) * tiles, last_tile), 0))
    const2 = lambda shape: pl.BlockSpec(shape, lambda b, j: (0,) * len(shape))
    weights = [const2((CONV_WIDTH, D_REC)), const2((1, D_REC)),
               const2((D_REC // LANES, LANES, 2 * LANES)), const2((2, D_REC)), const2((1, D_REC))]
    return row, prev, nxt, weights, nch


def _rg_lru(ux, ug, conv_w, conv_b, wg, bg, lam, batch, seq):
    t = ux.shape[0]
    row, prev, nxt, wspecs, nch = _rec_specs(batch, seq, False)
    hf = pl.pallas_call(
        _rec_fwd_kernel,
        out_shape=jax.ShapeDtypeStruct((t, D_REC), F32),
        grid=(batch, nch),
        in_specs=[prev, row, nxt] + wspecs,
        out_specs=row,
        scratch_shapes=[pltpu.VMEM((1, D_REC), F32)],
        compiler_params=_params(("arbitrary", "arbitrary")),
        name="rec_fwd",
    )(ux, ux, ux, conv_w, conv_b, wg[0], bg[0], lam[0:1])
    row, prev, nxt, wspecs, nch = _rec_specs(batch, seq, True)
    return pl.pallas_call(
        _rec_bwd_kernel,
        out_shape=jax.ShapeDtypeStruct((t, D_REC), F32),
        grid=(batch, nch),
        in_specs=[prev, row, nxt] + wspecs + [row, row],
        out_specs=row,
        scratch_shapes=[pltpu.VMEM((1, D_REC), F32), pltpu.VMEM((REC_CHUNK, D_REC), F32)],
        compiler_params=_params(("arbitrary", "arbitrary")),
        name="rec_bwd",
    )(ux, ux, ux, conv_w, conv_b, wg[1], bg[1], lam[1:2], hf, ug)


def _dft_a_kernel(zr_ref, zi_ref, t_ref, a_ref):
    for r in range(SUBLANES):
        rhs = jnp.concatenate([zr_ref[0, :, 0, r, :], zi_ref[0, :, 0, r, :]], axis=0)
        a_ref[0, 0, :, 0, r, :] = _dot(t_ref[r], rhs.astype(BF16))


def _dft_c_kernel(a_ref, m_ref, o_ref, *, scale):
    for j in range(SUBLANES):
        rhs = jnp.concatenate([a_ref[0, 0, 0, j], a_ref[0, 0, 1, j]], axis=0)
        o_ref[0, :, 0, j, :] = _dot(m_ref[...], rhs.astype(BF16)) * scale


def _dft_tables(seq):
    n2 = DFT_N2
    n1 = seq // n2
    k1 = jnp.arange(n1, dtype=jnp.int32)
    s = n2 * jnp.arange(n1, dtype=jnp.int32)[None, None, :] + jnp.arange(n2, dtype=jnp.int32)[:, None, None]
    ang = (2.0 * math.pi / seq) * ((k1[None, :, None] * s) % seq).astype(F32)
    cs, sn = jnp.cos(ang), jnp.sin(ang)
    t_a = jnp.concatenate([jnp.concatenate([cs, sn], axis=2),
                           jnp.concatenate([-sn, cs], axis=2)], axis=1).astype(BF16)
    k2 = jnp.arange(n2, dtype=jnp.int32)
    phi = (2.0 * math.pi / n2) * ((k2[:, None] * k2[None, :]) % n2).astype(F32)
    m_c = jnp.concatenate([jnp.cos(phi), jnp.sin(phi)], axis=1).astype(BF16)
    return t_a, m_c


def _seq_dft(zr, zi, batch, seq):
    n2 = DFT_N2
    n1 = seq // n2
    nj = n2 // SUBLANES
    ncb = D_FFT // LANES
    t_a, m_c = _dft_tables(seq)
    z5 = (batch, n1, nj, SUBLANES, D_FFT)
    zspec = pl.BlockSpec((1, n1, 1, SUBLANES, LANES), lambda b, cb, j: (b, 0, j, 0, cb))
    a6 = pl.pallas_call(
        _dft_a_kernel,
        out_shape=jax.ShapeDtypeStruct((batch, ncb, 2 * n1, nj, SUBLANES, LANES), F32),
        grid=(batch, ncb, nj),
        in_specs=[zspec, zspec,
                  pl.BlockSpec((SUBLANES, 2 * n1, 2 * n1), lambda b, cb, j: (j, 0, 0))],
        out_specs=pl.BlockSpec((1, 1, 2 * n1, 1, SUBLANES, LANES), lambda b, cb, j: (b, cb, 0, j, 0, 0)),
        compiler_params=_params(("arbitrary",) * 3),
        name="dft_a",
    )(zr.reshape(z5), zi.reshape(z5), t_a)
    ng = n1 // SUBLANES
    a6 = a6.reshape(batch, ncb, 2, ng, SUBLANES, n2, LANES)
    out = pl.pallas_call(
        functools.partial(_dft_c_kernel, scale=1.0 / math.sqrt(seq * FFT_GROUP_DIM)),
        out_shape=jax.ShapeDtypeStruct((batch, n2, ng, SUBLANES, D_FFT), F32),
        grid=(batch, ncb, ng),
        in_specs=[pl.BlockSpec((1, 1, 2, None, SUBLANES, n2, LANES), lambda b, cb, g: (b, cb, 0, g, 0, 0, 0)),
                  pl.BlockSpec((n2, 2 * n2), lambda b, cb, g: (0, 0))],
        out_specs=pl.BlockSpec((1, n2, 1, SUBLANES, LANES), lambda b, cb, g: (b, 0, g, 0, cb)),
        compiler_params=_params(("arbitrary",) * 3),
        name="dft_c",
    )(a6, m_c)
    return out.reshape(batch * seq, D_FFT)


ROUTE_TILE = 256


def _mix_route_kernel(rec_ref, four_ref, x_ref, mod_ref, gmix_ref, wout_ref, ln1g_ref, ln1b_ref,
                      wr_ref, br_ref, tri_ref,
                      x1_ref, h2_ref, tope_ref, topw_ref, rank_ref, cnt_ref, run_ref):
    i = pl.program_id(0)

    @pl.when(i == 0)
    def _():
        run_ref[...] = jnp.zeros_like(run_ref)

    m = mod_ref[0]
    gm = gmix_ref[...]
    rec = rec_ref[...]
    four = four_ref[...]
    rn = rec * lax.rsqrt(jnp.mean(rec * rec, axis=-1, keepdims=True) + LN_EPS) * gm[:, :D_REC]
    fn = four * lax.rsqrt(jnp.mean(four * four, axis=-1, keepdims=True) + LN_EPS) * gm[:, D_REC:]
    mix = _dot(rn.astype(BF16), wout_ref[:D_REC, :]) + _dot(fn.astype(BF16), wout_ref[D_REC:, :])
    x1 = _ln(DEEPNORM_ALPHA * x_ref[...] + m[2:3] * mix) * ln1g_ref[...] + ln1b_ref[...]
    x1_ref[...] = x1
    h2 = _ln(x1) * (1.0 + m[4:5]) + m[3:4]
    h2_ref[...] = h2
    logits = _dot3(h2, wr_ref[...]) + br_ref[...]
    lane = lax.broadcasted_iota(jnp.int32, logits.shape, 1).astype(F32)
    work = logits
    onehot = jnp.zeros(logits.shape, F32)
    vals, idxs = [], []
    for _ in range(TOP_K):
        v = jnp.max(work, axis=-1, keepdims=True)
        e = jnp.min(jnp.where(work == v, lane, float(N_EXPERTS)), axis=-1, keepdims=True)
        sel = lane == e
        onehot = jnp.where(sel, 1.0, onehot)
        work = jnp.where(sel, -jnp.inf, work)
        vals.append(v)
        idxs.append(e)
    ex = [jnp.exp(v - vals[0]) for v in vals]
    denom = ex[0] + ex[1] + ex[2] + ex[3]
    before = _dot(tri_ref[...], onehot.astype(BF16)) + run_ref[...]
    for k in range(TOP_K):
        tope_ref[:, k:k + 1] = idxs[k].astype(jnp.int32)
        topw_ref[:, k:k + 1] = ex[k] / denom
        rank_ref[:, k:k + 1] = jnp.sum(jnp.where(lane == idxs[k], before, 0.0), axis=-1,
                                       keepdims=True).astype(jnp.int32)
    run_ref[...] = run_ref[...] + jnp.sum(onehot, axis=0, keepdims=True)
    cnt_ref[...] = run_ref[...]


def _mix_route(rec, four, x2d, mod3, g_mix, w_out_bf, ln1_g, ln1_b, w_router, b_router, seq, b0):
    t = x2d.shape[0]
    tm = ROUTE_TILE
    tri = jnp.asarray(np.tril(np.ones((tm, tm), np.float32), -1), dtype=BF16)
    half = pl.BlockSpec((tm, D_REC), lambda i: (i, 0))
    full = pl.BlockSpec((tm, D_MODEL), lambda i: (i, 0))
    vec = pl.BlockSpec((1, D_MODEL), lambda i: (0, 0))
    k4 = pl.BlockSpec((tm, TOP_K), lambda i: (i, 0))
    return pl.pallas_call(
        _mix_route_kernel,
        out_shape=[jax.ShapeDtypeStruct((t, D_MODEL), F32), jax.ShapeDtypeStruct((t, D_MODEL), F32),
                   jax.ShapeDtypeStruct((t, TOP_K), jnp.int32), jax.ShapeDtypeStruct((t, TOP_K), F32),
                   jax.ShapeDtypeStruct((t, TOP_K), jnp.int32), jax.ShapeDtypeStruct((1, N_EXPERTS), F32)],
        grid=(t // tm,),
        in_specs=[half, half, full,
                  pl.BlockSpec((1, 6, D_MODEL), lambda i: (b0 + (i * tm) // seq, 0, 0)),
                  vec, pl.BlockSpec((D_MODEL, D_MODEL), lambda i: (0, 0)), vec, vec,
                  pl.BlockSpec((D_MODEL, N_EXPERTS), lambda i: (0, 0)),
                  pl.BlockSpec((1, N_EXPERTS), lambda i: (0, 0)),
                  pl.BlockSpec((tm, tm), lambda i: (0, 0))],
        out_specs=[full, full, k4, k4, k4, pl.BlockSpec((1, N_EXPERTS), lambda i: (0, 0))],
        scratch_shapes=[pltpu.VMEM((1, N_EXPERTS), F32)],
        compiler_params=_params(("arbitrary",)),
        name="mix_route",
    )(rec, four, x2d, mod3, g_mix.reshape(1, -1), w_out_bf, ln1_g.reshape(1, -1), ln1_b.reshape(1, -1),
      w_router, b_router.reshape(1, -1), tri)


MOE_TILE = 256


def _moe_kernel(blk_e_ref, nact_ref, tok_hbm, h_hbm, wgu_ref, bgu_ref, wd_ref, bd_ref, o_ref,
                idx_ref, xbuf_ref, idx_sem, row_sem):
    i = pl.program_id(0)
    nb = pl.num_programs(0)
    nact = nact_ref[0]
    tm = MOE_TILE

    def idx_copy(blk, slot):
        return pltpu.make_async_copy(tok_hbm.at[blk], idx_ref.at[slot], idx_sem.at[slot])

    def start_rows(slot):
        def body(r, _):
            tok = idx_ref[slot, r]
            pltpu.make_async_copy(h_hbm.at[pl.ds(tok, 1)], xbuf_ref.at[slot, pl.ds(r, 1)],
                                  row_sem.at[slot]).start()
            return 0
        lax.fori_loop(0, tm, body, 0)

    def wait_rows(slot):
        pltpu.make_async_copy(h_hbm.at[pl.ds(0, tm)], xbuf_ref.at[slot], row_sem.at[slot]).wait()

    slot = i % 2

    @pl.when(jnp.logical_and(i == 0, nact > 0))
    def _():
        cp = idx_copy(0, 0)
        cp.start()
        cp.wait()
        start_rows(0)

    @pl.when(i + 1 < nact)
    def _():
        cp = idx_copy(i + 1, 1 - slot)
        cp.start()
        cp.wait()
        start_rows(1 - slot)

    @pl.when(i < nact)
    def _():
        wait_rows(slot)
        xb = xbuf_ref[slot].astype(BF16)
        gu = _dot(xb, wgu_ref[0]) + bgu_ref[0]
        gate = jnp.minimum(gu[:, :D_FF], SWIGLU_LIMIT)
        up = jnp.clip(gu[:, D_FF:], -SWIGLU_LIMIT, SWIGLU_LIMIT)
        act = (up + 1.0) * (gate * jax.nn.sigmoid(SWIGLU_ALPHA * gate))
        o_ref[...] = _dot(act.astype(BF16), wd_ref[0]) + bd_ref[0]

    @pl.when(i >= nact)
    def _():
        o_ref[...] = jnp.zeros_like(o_ref)


def _moe(h2, slot_tok, blk_e, nact, wgu_bf, b_gate_up, wd_bf, b_down):
    n_blocks = blk_e.shape[0]
    tm = MOE_TILE
    grid_spec = pltpu.PrefetchScalarGridSpec(
        num_scalar_prefetch=2,
        grid=(n_blocks,),
        in_specs=[pl.BlockSpec(memory_space=pl.ANY),
                  pl.BlockSpec(memory_space=pl.ANY),
                  pl.BlockSpec((1, D_MODEL, 2 * D_FF), lambda i, be, na: (be[i], 0, 0)),
                  pl.BlockSpec((1, 1, 2 * D_FF), lambda i, be, na: (be[i], 0, 0)),
                  pl.BlockSpec((1, D_FF, D_MODEL), lambda i, be, na: (be[i], 0, 0)),
                  pl.BlockSpec((1, 1, D_MODEL), lambda i, be, na: (be[i], 0, 0))],
        out_specs=pl.BlockSpec((tm, D_MODEL), lambda i, be, na: (i, 0)),
        scratch_shapes=[pltpu.SMEM((2, tm), jnp.int32),
                        pltpu.VMEM((2, tm, D_MODEL), F32),
                        pltpu.SemaphoreType.DMA((2,)),
                        pltpu.SemaphoreType.DMA((2,))],
    )
    return pl.pallas_call(
        _moe_kernel,
        out_shape=jax.ShapeDtypeStruct((n_blocks * tm, D_MODEL), F32),
        grid_spec=grid_spec,
        compiler_params=_params(("arbitrary",)),
        name="moe",
    )(blk_e, nact, slot_tok.reshape(n_blocks, tm), h2, wgu_bf, b_gate_up.reshape(N_EXPERTS, 1, -1),
      wd_bf, b_down.reshape(N_EXPERTS, 1, -1))


COMBINE_TILE = 256


def _combine_kernel(dest_hbm, outs_hbm, x1_ref, w_ref, mod_ref, g_ref, b_ref, o_ref,
                    idx_ref, buf_ref, idx_sem, row_sem):
    i = pl.program_id(0)
    nb = pl.num_programs(0)
    tk = COMBINE_TILE

    def idx_copy(blk, slot):
        return pltpu.make_async_copy(dest_hbm.at[blk], idx_ref.at[slot], idx_sem.at[slot])

    def start_rows(slot):
        def body(r, _):
            for k in range(TOP_K):
                d = idx_ref[slot, k * tk + r]
                pltpu.make_async_copy(outs_hbm.at[pl.ds(d, 1)], buf_ref.at[slot, k, pl.ds(r, 1)],
                                      row_sem.at[slot]).start()
            return 0
        lax.fori_loop(0, tk, body, 0)

    def wait_rows(slot):
        for k in range(TOP_K):
            pltpu.make_async_copy(outs_hbm.at[pl.ds(0, tk)], buf_ref.at[slot, k], row_sem.at[slot]).wait()

    slot = i % 2

    @pl.when(i == 0)
    def _():
        cp = idx_copy(0, 0)
        cp.start()
        cp.wait()
        start_rows(0)

    @pl.when(i + 1 < nb)
    def _():
        cp = idx_copy(i + 1, 1 - slot)
        cp.start()
        cp.wait()
        start_rows(1 - slot)

    wait_rows(slot)
    w = w_ref[...]
    y = w[:, 0:1] * buf_ref[slot, 0]
    for k in range(1, TOP_K):
        y = y + w[:, k:k + 1] * buf_ref[slot, k]
    m = mod_ref[0]
    o_ref[...] = _ln(DEEPNORM_ALPHA * x1_ref[...] + m[5:6] * y) * g_ref[...] + b_ref[...]


def _combine(dest_km, outs, x1, top_w, mod3, ln2_g, ln2_b, seq, b0):
    t = x1.shape[0]
    tk = COMBINE_TILE
    full = pl.BlockSpec((tk, D_MODEL), lambda i: (i, 0))
    vec = pl.BlockSpec((1, D_MODEL), lambda i: (0, 0))
    return pl.pallas_call(
        _combine_kernel,
        out_shape=jax.ShapeDtypeStruct((t, D_MODEL), F32),
        grid=(t // tk,),
        in_specs=[pl.BlockSpec(memory_space=pl.ANY), pl.BlockSpec(memory_space=pl.ANY),
                  full, pl.BlockSpec((tk, TOP_K), lambda i: (i, 0)),
                  pl.BlockSpec((1, 6, D_MODEL), lambda i: (b0 + (i * tk) // seq, 0, 0)),
                  vec, vec],
        out_specs=full,
        scratch_shapes=[pltpu.SMEM((2, TOP_K * tk), jnp.int32),
                        pltpu.VMEM((2, TOP_K, tk, D_MODEL), F32),
                        pltpu.SemaphoreType.DMA((2,)),
                        pltpu.SemaphoreType.DMA((2,))],
        compiler_params=_params(("arbitrary",)),
        name="combine",
    )(dest_km, outs, x1, top_w, mod3, ln2_g.reshape(1, -1), ln2_b.reshape(1, -1))


def _gate_weights(w_rg, b_rg):
    hpb = LANES // REC_HEAD_DIM
    nblk = D_REC // LANES
    w = w_rg.reshape(2, 2, nblk, hpb, REC_HEAD_DIM, REC_HEAD_DIM)
    eye = jnp.eye(hpb, dtype=w.dtype)
    blk = jnp.einsum('dgbhij,hk->dgbhikj', w, eye).reshape(2, 2, nblk, LANES, LANES)
    wg = jnp.concatenate([blk[:, 0], blk[:, 1]], axis=-1).astype(BF16)
    bg = b_rg.reshape(2, 2, D_REC)
    return wg, bg


def _routing_tables(top_e, rank, counts, tm):
    t = top_e.shape[0]
    n_asg = t * TOP_K
    counts = counts.reshape(N_EXPERTS).astype(jnp.int32)
    padded = (counts + tm - 1) // tm * tm
    pad_ends = jnp.cumsum(padded)
    pad_starts = pad_ends - padded
    dest = pad_starts[top_e] + rank
    n_blocks = n_asg // tm + N_EXPERTS
    tok = jnp.broadcast_to(jnp.arange(t, dtype=jnp.int32)[:, None], (t, TOP_K))
    slot_tok = jnp.zeros((n_blocks * tm,), jnp.int32).at[dest.reshape(-1)].set(tok.reshape(-1))
    blk_e = jnp.minimum(jnp.searchsorted(pad_ends, jnp.arange(n_blocks, dtype=jnp.int32) * tm, side='right'),
                        N_EXPERTS - 1).astype(jnp.int32)
    nact = (pad_ends[-1] // tm).astype(jnp.int32).reshape(1)
    return dest, slot_tok, blk_e, nact


def _trunk(x, mod3, b0, p):
    batch, seq, _ = x.shape
    t = batch * seq
    x2d = x.reshape(t, D_MODEL)
    ux, ug, zr, zi = _inproj(x2d, mod3, p['w_in_bf'], seq, b0)
    rec = _rg_lru(ux, ug, p['conv_w'], p['conv_b'], p['wg'], p['bg'], p['lam'], batch, seq)
    four = _seq_dft(zr, zi, batch, seq)
    x1, h2, top_e, top_w, rank, counts = _mix_route(rec, four, x2d, mod3, p['g_mix'], p['w_out_bf'],
                                                    p['ln1_g'], p['ln1_b'], p['w_router'], p['b_router'],
                                                    seq, b0)
    dest, slot_tok, blk_e, nact = _routing_tables(top_e, rank, counts, MOE_TILE)
    outs = _moe(h2, slot_tok, blk_e, nact, p['wgu_bf'], p['b_gate_up'], p['wd_bf'], p['b_down'])
    tk = COMBINE_TILE
    dest_km = dest.reshape(t // tk, tk, TOP_K).transpose(0, 2, 1).reshape(t // tk, TOP_K * tk)
    y = _combine(dest_km, outs, x1, top_w, mod3, p['ln2_g'], p['ln2_b'], seq, b0)
    return y.reshape(batch, seq, D_MODEL)


def kernel(x_prompt, x_sample, c_prompt, c_sample, w_ada, b_ada, w_in, conv_w, conv_b, w_rg, b_rg,
           lru_lambda, g_mix, w_out, ln1_g, ln1_b, w_router, b_router, w_gate_up, b_gate_up,
           w_down, b_down, ln2_g, ln2_b):
    depth = w_ada.shape[0]
    xs = [x_prompt, x_sample]
    cs = [c_prompt, c_sample]
    nb = [c.shape[0] for c in cs]
    bp = -(-sum(nb) // SUBLANES) * SUBLANES
    for l in range(depth):
        wg, bg = _gate_weights(w_rg[l], b_rg[l])
        p = dict(w_in_bf=w_in[l].astype(BF16), conv_w=conv_w[l], conv_b=conv_b[l].reshape(1, -1),
                 wg=wg, bg=bg, lam=lru_lambda[l], g_mix=g_mix[l], w_out_bf=w_out[l].astype(BF16),
                 ln1_g=ln1_g[l], ln1_b=ln1_b[l], w_router=w_router[l], b_router=b_router[l],
                 wgu_bf=w_gate_up[l].astype(BF16), b_gate_up=b_gate_up[l],
                 wd_bf=w_down[l].astype(BF16), b_down=b_down[l], ln2_g=ln2_g[l], ln2_b=ln2_b[l])
        c_all = jnp.concatenate(cs + [jnp.zeros((bp - sum(nb), D_MODEL), F32)], axis=0)
        mod3 = _ada_mod(c_all, w_ada[l], b_ada[l]).reshape(bp, 6, D_MODEL)
        xs = [_trunk(xs[0], mod3, 0, p), _trunk(xs[1], mod3, nb[0], p)]
    return (xs[0], xs[1])
```

```python
import functools
import math

import numpy as np
import jax
import jax.numpy as jnp
from jax import lax
from jax.experimental import pallas as pl
from jax.experimental.pallas import tpu as pltpu

D_MODEL = 1024
D_REC = 512
D_FFT = 512
N_REC_HEADS = 8
REC_HEAD_DIM = 64
FFT_GROUP_DIM = 128
N_FFT_GROUPS = 4
CONV_WIDTH = 4
CONV_LEFT = 2
LRU_C = 8.0
N_EXPERTS = 32
TOP_K = 4
D_FF = 1024
SWIGLU_LIMIT = 7.0
SWIGLU_ALPHA = 1.702
DEEPNORM_ALPHA = 2.0 ** 0.25
LN_EPS = 1e-5

LANES = 128
SUBLANES = 8
ROW_SUB = D_MODEL // LANES
assert ROW_SUB == SUBLANES
DFT_N2 = 128
VMEM_LIMIT = 48 * 1024 * 1024

F32 = jnp.float32
BF16 = jnp.bfloat16


def _lane_block(j, n):
    return (pl.ds(j, n, stride=ROW_SUB), slice(None))


def _params(sem, vmem=VMEM_LIMIT):
    return pltpu.CompilerParams(dimension_semantics=sem, vmem_limit_bytes=vmem)


def _dot(a, b):
    return jnp.dot(a, b, preferred_element_type=F32)


def _dot3(a, b):
    a_hi = a.astype(BF16)
    b_hi = b.astype(BF16)
    a_lo = (a - a_hi.astype(F32)).astype(BF16)
    b_lo = (b - b_hi.astype(F32)).astype(BF16)
    return _dot(a_hi, b_hi) + _dot(a_lo, b_hi) + _dot(a_hi, b_lo)


def _ln(x):
    mu = jnp.mean(x, axis=-1, keepdims=True)
    xc = x - mu
    var = jnp.mean(xc * xc, axis=-1, keepdims=True)
    return xc * lax.rsqrt(var + LN_EPS)


def _ada_kernel(c_ref, w_ref, b_ref, o_ref):
    c = c_ref[...]
    s = c * jax.nn.sigmoid(c)
    o_ref[...] = _dot3(s, w_ref[...]) + b_ref[...]


def _ada_mod(c_all, w_ada, b_ada):
    bp = c_all.shape[0]
    n = w_ada.shape[1]
    tn = 768
    return pl.pallas_call(
        _ada_kernel,
        out_shape=jax.ShapeDtypeStruct((bp, n), F32),
        grid=(n // tn,),
        in_specs=[pl.BlockSpec((bp, D_MODEL), lambda j: (0, 0)),
                  pl.BlockSpec((D_MODEL, tn), lambda j: (0, j)),
                  pl.BlockSpec((1, tn), lambda j: (0, j))],
        out_specs=pl.BlockSpec((bp, tn), lambda j: (0, j)),
        compiler_params=_params(("arbitrary",)),
        name="ada_mod",
    )(c_all, w_ada, b_ada.reshape(1, n))


def _inproj_kernel(x_ref, mod_ref, w_ref, dft_ref, ux_ref, ug_ref, zr_ref, zi_ref):
    m = mod_ref[0]
    h = _ln(x_ref[...]) * (1.0 + m[1:2]) + m[0:1]
    u = _dot(h.astype(BF16), w_ref[...])
    ux_ref[...] = u[:, :D_REC]
    ug_ref[...] = u[:, D_REC:2 * D_REC]
    for g in range(N_FFT_GROUPS):
        lo = 2 * D_REC + g * FFT_GROUP_DIM
        z = _dot(u[:, lo:lo + FFT_GROUP_DIM].astype(BF16), dft_ref[...])
        zr_ref[:, g * FFT_GROUP_DIM:(g + 1) * FFT_GROUP_DIM] = z[:, :FFT_GROUP_DIM]
        zi_ref[:, g * FFT_GROUP_DIM:(g + 1) * FFT_GROUP_DIM] = z[:, FFT_GROUP_DIM:]


def _group_dft_matrix():
    j = np.arange(FFT_GROUP_DIM)
    ang = 2.0 * np.pi * ((j[:, None] * j[None, :]) % FFT_GROUP_DIM) / FFT_GROUP_DIM
    return jnp.asarray(np.concatenate([np.cos(ang), -np.sin(ang)], axis=1), dtype=F32).astype(BF16)


def _inproj(x2d, mod3, w_in_bf, seq, b0):
    t = x2d.shape[0]
    tm = 512
    row = pl.BlockSpec((tm, D_REC), lambda i: (i, 0))
    return pl.pallas_call(
        _inproj_kernel,
        out_shape=[jax.ShapeDtypeStruct((t, D_REC), F32)] * 4,
        grid=(t // tm,),
        in_specs=[pl.BlockSpec((tm, D_MODEL), lambda i: (i, 0)),
                  pl.BlockSpec((1, 6, D_MODEL), lambda i: (b0 + (i * tm) // seq, 0, 0)),
                  pl.BlockSpec((D_MODEL, 3 * D_REC), lambda i: (0, 0)),
                  pl.BlockSpec((FFT_GROUP_DIM, 2 * FFT_GROUP_DIM), lambda i: (0, 0))],
        out_specs=[row, row, row, row],
        compiler_params=_params(("arbitrary",)),
        name="inproj",
    )(x2d, mod3, w_in_bf, _group_dft_matrix())


REC_CHUNK = 256


def _scan_chunk(a, b, carry, h_ref, reverse):
    rows, width = a.shape
    n_tiles = rows // SUBLANES
    a = a.reshape(n_tiles, SUBLANES, width)
    b = b.reshape(n_tiles, SUBLANES, width)
    ridx = lax.broadcasted_iota(jnp.int32, a.shape, 1)
    for s in (1, 2, 4):
        shift = SUBLANES - s if reverse else s
        keep = ridx < SUBLANES - s if reverse else ridx >= s
        a_sh = pltpu.roll(a, shift, 1)
        b_sh = pltpu.roll(b, shift, 1)
        b = jnp.where(keep, a * b_sh + b, b)
        a = jnp.where(keep, a * a_sh, a)
    order = range(n_tiles - 1, -1, -1) if reverse else range(n_tiles)
    for j in order:
        h = b[j] + a[j] * carry
        carry = h[0:1] if reverse else h[SUBLANES - 1:SUBLANES]
        h_ref[j * SUBLANES:(j + 1) * SUBLANES, :] = h
    return carry


def _conv_chunk(prev_ref, x_ref, next_ref, cw_ref, cb_ref, first, last):
    rows = x_ref.shape[0]
    prev = jnp.where(first, 0.0, prev_ref[...])
    nxt = jnp.where(last, 0.0, next_ref[...])
    ext = jnp.concatenate([prev, x_ref[...], nxt], axis=0)
    n = rows + 2 * SUBLANES
    body = slice(SUBLANES, SUBLANES + rows)
    cw = cw_ref[...]
    xc = cw[2:3] * ext[body]
    xc = xc + cw[0:1] * pltpu.roll(ext, 2, 0)[body]
    xc = xc + cw[1:2] * pltpu.roll(ext, 1, 0)[body]
    xc = xc + cw[3:4] * pltpu.roll(ext, n - 1, 0)[body]
    return xc + cb_ref[...]


def _sigmoid(x):
    return 0.5 * jnp.tanh(0.5 * x) + 0.5


def _neg_expm1_of_double(half_y):
    t = jnp.tanh(half_y)
    return (-2.0 * t) / (1.0 - t)


def _lru_coeffs(xc, wg_ref, bg_ref, lam_ref):
    lam = lam_ref[...]
    neg = -lam
    softplus = jnp.maximum(neg, 0.0) + jnp.log(1.0 + jnp.exp(-jnp.abs(neg)))
    xb = xc.astype(BF16)
    r_parts, i_parts = [], []
    for cb in range(D_REC // LANES):
        g = _dot(xb[:, cb * LANES:(cb + 1) * LANES], wg_ref[cb])
        r_parts.append(g[:, :LANES])
        i_parts.append(g[:, LANES:])
    bg = bg_ref[...]
    r = _sigmoid(jnp.concatenate(r_parts, axis=1) + bg[0:1])
    i = _sigmoid(jnp.concatenate(i_parts, axis=1) + bg[1:2])
    log_a = (-LRU_C) * r * softplus
    a = jnp.exp(log_a)
    mult = jnp.sqrt(_neg_expm1_of_double(log_a))
    return a, mult * (i * xc)


def _rec_fwd_kernel(prev_ref, x_ref, next_ref, cw_ref, cb_ref, wg_ref, bg_ref, lam_ref,
                    hf_ref, carry_ref):
    c = pl.program_id(1)
    nch = pl.num_programs(1)

    @pl.when(c == 0)
    def _():
        carry_ref[...] = jnp.zeros_like(carry_ref)

    xc = _conv_chunk(prev_ref, x_ref, next_ref, cw_ref, cb_ref, c == 0, c == nch - 1)
    a, b = _lru_coeffs(xc, wg_ref, bg_ref, lam_ref)
    carry_ref[...] = _scan_chunk(a, b, carry_ref[...], hf_ref, reverse=False)


def _rec_bwd_kernel(prev_ref, x_ref, next_ref, cw_ref, cb_ref, wg_ref, bg_ref, lam_ref,
                    hf_ref, ug_ref, rec_ref, carry_ref, hb_ref):
    j = pl.program_id(1)
    nch = pl.num_programs(1)

    @pl.when(j == 0)
    def _():
        carry_ref[...] = jnp.zeros_like(carry_ref)

    xc = _conv_chunk(prev_ref, x_ref, next_ref, cw_ref, cb_ref, j == nch - 1, j == 0)
    a, b = _lru_coeffs(xc, wg_ref, bg_ref, lam_ref)
    carry_ref[...] = _scan_chunk(a, b, carry_ref[...], hb_ref, reverse=True)
    ug = ug_ref[...]
    gelu = 0.5 * ug * (1.0 + jnp.tanh(math.sqrt(2.0 / math.pi) * (ug + 0.044715 * (ug * ug * ug))))
    rec_ref[...] = (hf_ref[...] + hb_ref[...]) * gelu


def _rec_specs(batch, seq, reverse):
    lc = REC_CHUNK
    nch = seq // lc
    tiles = lc // SUBLANES
    last_tile = batch * nch * tiles - 1

    def chunk(b, j):
        return nch - 1 - j if reverse else j

    row = pl.BlockSpec((lc, D_REC), lambda b, j: (b * nch + chunk(b, j), 0))
    prev = pl.BlockSpec((SUBLANES, D_REC),
                        lambda b, j: (jnp.maximum((b * nch + chunk(b, j)) * tiles - 1, 0), 0))
    nxt = pl.BlockSpec((SUBLANES, D_REC),
                       lambda b, j: (jnp.minimum((b * nch + chunk(b, j) + 1) * tiles, last_tile), 0))
    const2 = lambda shape: pl.BlockSpec(shape, lambda b, j: (0,) * len(shape))
    weights = [const2((CONV_WIDTH, D_REC)), const2((1, D_REC)),
               const2((D_REC // LANES, LANES, 2 * LANES)), const2((2, D_REC)), const2((1, D_REC))]
    return row, prev, nxt, weights, nch


def _rg_lru(ux, ug, conv_w, conv_b, wg, bg, lam, batch, seq):
    t = ux.shape[0]
    row, prev, nxt, wspecs, nch = _rec_specs(batch, seq, False)
    hf = pl.pallas_call(
        _rec_fwd_kernel,
        out_shape=jax.ShapeDtypeStruct((t, D_REC), F32),
        grid=(batch, nch),
        in_specs=[prev, row, nxt] + wspecs,
        out_specs=row,
        scratch_shapes=[pltpu.VMEM((1, D_REC), F32)],
        compiler_params=_params(("arbitrary", "arbitrary")),
        name="rec_fwd",
    )(ux, ux, ux, conv_w, conv_b, wg[0], bg[0], lam[0:1])
    row, prev, nxt, wspecs, nch = _rec_specs(batch, seq, True)
    return pl.pallas_call(
        _rec_bwd_kernel,
        out_shape=jax.ShapeDtypeStruct((t, D_REC), F32),
        grid=(batch, nch),
        in_specs=[prev, row, nxt] + wspecs + [row, row],
        out_specs=row,
        scratch_shapes=[pltpu.VMEM((1, D_REC), F32), pltpu.VMEM((REC_CHUNK, D_REC), F32)],
        compiler_params=_params(("arbitrary", "arbitrary")),
        name="rec_bwd",
    )(ux, ux, ux, conv_w, conv_b, wg[1], bg[1], lam[1:2], hf, ug)


DFT_A_JB = 8
DFT_C_GB = 2


def _dft_a_kernel(zr_ref, zi_ref, t_ref, a_ref):
    for jb in range(zr_ref.shape[2]):
        for r in range(SUBLANES):
            rhs = jnp.concatenate([zr_ref[0, :, jb, r, :], zi_ref[0, :, jb, r, :]], axis=0)
            a_ref[0, 0, :, jb, r, :] = _dot(t_ref[jb * SUBLANES + r], rhs.astype(BF16))


def _dft_c_kernel(a_ref, m_ref, o_ref, *, scale):
    for g in range(a_ref.shape[3]):
        for j in range(SUBLANES):
            rhs = jnp.concatenate([a_ref[0, 0, 0, g, j], a_ref[0, 0, 1, g, j]], axis=0)
            o_ref[0, :, g, j, :] = _dot(m_ref[...], rhs.astype(BF16)) * scale


def _dft_tables(seq):
    n2 = DFT_N2
    n1 = seq // n2
    k1 = jnp.arange(n1, dtype=jnp.int32)
    s = n2 * jnp.arange(n1, dtype=jnp.int32)[None, None, :] + jnp.arange(n2, dtype=jnp.int32)[:, None, None]
    ang = (2.0 * math.pi / seq) * ((k1[None, :, None] * s) % seq).astype(F32)
    cs, sn = jnp.cos(ang), jnp.sin(ang)
    t_a = jnp.concatenate([jnp.concatenate([cs, sn], axis=2),
                           jnp.concatenate([-sn, cs], axis=2)], axis=1).astype(BF16)
    k2 = jnp.arange(n2, dtype=jnp.int32)
    phi = (2.0 * math.pi / n2) * ((k2[:, None] * k2[None, :]) % n2).astype(F32)
    m_c = jnp.concatenate([jnp.cos(phi), jnp.sin(phi)], axis=1).astype(BF16)
    return t_a, m_c


def _seq_dft(zr, zi, batch, seq):
    n2 = DFT_N2
    n1 = seq // n2
    nj = n2 // SUBLANES
    ncb = D_FFT // LANES
    t_a, m_c = _dft_tables(seq)
    z5 = (batch, n1, nj, SUBLANES, D_FFT)
    jb = min(DFT_A_JB, nj)
    zspec = pl.BlockSpec((1, n1, jb, SUBLANES, LANES), lambda b, cb, j: (b, 0, j, 0, cb))
    a6 = pl.pallas_call(
        _dft_a_kernel,
        out_shape=jax.ShapeDtypeStruct((batch, ncb, 2 * n1, nj, SUBLANES, LANES), F32),
        grid=(batch, ncb, nj // jb),
        in_specs=[zspec, zspec,
                  pl.BlockSpec((jb * SUBLANES, 2 * n1, 2 * n1), lambda b, cb, j: (j, 0, 0))],
        out_specs=pl.BlockSpec((1, 1, 2 * n1, jb, SUBLANES, LANES), lambda b, cb, j: (b, cb, 0, j, 0, 0)),
        compiler_params=_params(("arbitrary",) * 3),
        name="dft_a",
    )(zr.reshape(z5), zi.reshape(z5), t_a)
    ng = n1 // SUBLANES
    gb = min(DFT_C_GB, ng)
    a6 = a6.reshape(batch, ncb, 2, ng, SUBLANES, n2, LANES)
    out = pl.pallas_call(
        functools.partial(_dft_c_kernel, scale=1.0 / math.sqrt(seq * FFT_GROUP_DIM)),
        out_shape=jax.ShapeDtypeStruct((batch, n2, ng, SUBLANES, D_FFT), F32),
        grid=(batch, ncb, ng // gb),
        in_specs=[pl.BlockSpec((1, 1, 2, gb, SUBLANES, n2, LANES), lambda b, cb, g: (b, cb, 0, g, 0, 0, 0)),
                  pl.BlockSpec((n2, 2 * n2), lambda b, cb, g: (0, 0))],
        out_specs=pl.BlockSpec((1, n2, gb, SUBLANES, LANES), lambda b, cb, g: (b, 0, g, 0, cb)),
        compiler_params=_params(("arbitrary",) * 3),
        name="dft_c",
    )(a6, m_c)
    return out.reshape(batch * seq, D_FFT)


ROUTE_TILE = 256


def _mix_route_kernel(rec_ref, four_ref, x_ref, mod_ref, gmix_ref, wout_ref, ln1g_ref, ln1b_ref,
                      wr_ref, br_ref, tri_ref,
                      x1_ref, h2_ref, tope_ref, topw_ref, rank_ref, cnt_ref, run_ref):
    i = pl.program_id(0)

    @pl.when(i == 0)
    def _():
        run_ref[...] = jnp.zeros_like(run_ref)

    m = mod_ref[0]
    gm = gmix_ref[...]
    rec = rec_ref[...]
    four = four_ref[...]
    rn = rec * lax.rsqrt(jnp.mean(rec * rec, axis=-1, keepdims=True) + LN_EPS) * gm[:, :D_REC]
    fn = four * lax.rsqrt(jnp.mean(four * four, axis=-1, keepdims=True) + LN_EPS) * gm[:, D_REC:]
    mix = _dot(rn.astype(BF16), wout_ref[:D_REC, :]) + _dot(fn.astype(BF16), wout_ref[D_REC:, :])
    x1 = _ln(DEEPNORM_ALPHA * x_ref[...] + m[2:3] * mix) * ln1g_ref[...] + ln1b_ref[...]
    x1_ref[...] = x1
    h2 = _ln(x1) * (1.0 + m[4:5]) + m[3:4]
    for j in range(SUBLANES):
        h2_ref[_lane_block(j, h2.shape[0])] = h2[:, j * LANES:(j + 1) * LANES]
    logits = _dot3(h2, wr_ref[...]) + br_ref[...]
    lane = lax.broadcasted_iota(jnp.int32, logits.shape, 1).astype(F32)
    work = logits
    onehot = jnp.zeros(logits.shape, F32)
    vals, idxs = [], []
    for _ in range(TOP_K):
        v = jnp.max(work, axis=-1, keepdims=True)
        e = jnp.min(jnp.where(work == v, lane, float(N_EXPERTS)), axis=-1, keepdims=True)
        sel = lane == e
        onehot = jnp.where(sel, 1.0, onehot)
        work = jnp.where(sel, -jnp.inf, work)
        vals.append(v)
        idxs.append(e)
    ex = [jnp.exp(v - vals[0]) for v in vals]
    denom = ex[0] + ex[1] + ex[2] + ex[3]
    before = _dot(tri_ref[...], onehot.astype(BF16)) + run_ref[...]
    for k in range(TOP_K):
        tope_ref[:, k:k + 1] = idxs[k].astype(jnp.int32)
        topw_ref[:, k:k + 1] = ex[k] / denom
        rank_ref[:, k:k + 1] = jnp.sum(jnp.where(lane == idxs[k], before, 0.0), axis=-1,
                                       keepdims=True).astype(jnp.int32)
    run_ref[...] = run_ref[...] + jnp.sum(onehot, axis=0, keepdims=True)
    cnt_ref[...] = run_ref[...]


def _mix_route(rec, four, x2d, mod3, g_mix, w_out_bf, ln1_g, ln1_b, w_router, b_router, seq, b0):
    t = x2d.shape[0]
    tm = ROUTE_TILE
    tri = jnp.asarray(np.tril(np.ones((tm, tm), np.float32), -1), dtype=BF16)
    half = pl.BlockSpec((tm, D_REC), lambda i: (i, 0))
    full = pl.BlockSpec((tm, D_MODEL), lambda i: (i, 0))
    vec = pl.BlockSpec((1, D_MODEL), lambda i: (0, 0))
    k4 = pl.BlockSpec((tm, TOP_K), lambda i: (i, 0))
    return pl.pallas_call(
        _mix_route_kernel,
        out_shape=[jax.ShapeDtypeStruct((t, D_MODEL), F32), jax.ShapeDtypeStruct((t * ROW_SUB, LANES), F32),
                   jax.ShapeDtypeStruct((t, TOP_K), jnp.int32), jax.ShapeDtypeStruct((t, TOP_K), F32),
                   jax.ShapeDtypeStruct((t, TOP_K), jnp.int32), jax.ShapeDtypeStruct((1, N_EXPERTS), F32)],
        grid=(t // tm,),
        in_specs=[half, half, full,
                  pl.BlockSpec((1, 6, D_MODEL), lambda i: (b0 + (i * tm) // seq, 0, 0)),
                  vec, pl.BlockSpec((D_MODEL, D_MODEL), lambda i: (0, 0)), vec, vec,
                  pl.BlockSpec((D_MODEL, N_EXPERTS), lambda i: (0, 0)),
                  pl.BlockSpec((1, N_EXPERTS), lambda i: (0, 0)),
                  pl.BlockSpec((tm, tm), lambda i: (0, 0))],
        out_specs=[full, pl.BlockSpec((tm * ROW_SUB, LANES), lambda i: (i, 0)), k4, k4, k4,
                   pl.BlockSpec((1, N_EXPERTS), lambda i: (0, 0))],
        scratch_shapes=[pltpu.VMEM((1, N_EXPERTS), F32)],
        compiler_params=_params(("arbitrary",)),
        name="mix_route",
    )(rec, four, x2d, mod3, g_mix.reshape(1, -1), w_out_bf, ln1_g.reshape(1, -1), ln1_b.reshape(1, -1),
      w_router, b_router.reshape(1, -1), tri)


MOE_TILE = 256


IDX_RING = 4


def _moe_kernel(blk_e_ref, nact_ref, asg_hbm, h_hbm, wgu_ref, bgu_ref, wd_ref, bd_ref, out_hbm,
                idx_ref, xbuf_ref, obuf_ref, xb_ref, idx_sem, row_sem, sc_sem, *, n_tok):
    i = pl.program_id(0)
    nact = nact_ref[0]
    tm = MOE_TILE
    last = nact - 1

    def idx_copy(blk, ring):
        return pltpu.make_async_copy(asg_hbm.at[blk], idx_ref.at[ring], idx_sem.at[ring])

    def gather_rows(ring, slot):
        for r in range(tm):
            tok = idx_ref[ring, r] & (n_tok - 1)
            src = pl.multiple_of(tok * ROW_SUB, ROW_SUB)
            pltpu.make_async_copy(h_hbm.at[pl.ds(src, ROW_SUB)], xbuf_ref.at[slot, pl.ds(r * ROW_SUB, ROW_SUB)],
                                  row_sem.at[slot]).start(priority=r % 2)

    def scatter_rows(ring, slot):
        for r in range(tm):
            dst = idx_ref[ring, r]
            row = pl.multiple_of(dst * ROW_SUB, ROW_SUB)
            pltpu.make_async_copy(obuf_ref.at[slot, pl.ds(r * ROW_SUB, ROW_SUB)], out_hbm.at[pl.ds(row, ROW_SUB)],
                                  sc_sem.at[slot]).start(priority=r % 2)

    def wait_gather(slot):
        pltpu.make_async_copy(h_hbm.at[pl.ds(0, tm * ROW_SUB)], xbuf_ref.at[slot], row_sem.at[slot]).wait()

    def wait_scatter(slot):
        pltpu.make_async_copy(obuf_ref.at[slot], out_hbm.at[pl.ds(0, tm * ROW_SUB)], sc_sem.at[slot]).wait()

    def step(with_scatter):
        slot = i % 2
        wait_gather(slot)
        idx_copy(0, (i + 1) % IDX_RING).wait()
        idx_copy(jnp.minimum(i + 2, last), (i + 2) % IDX_RING).start()
        if with_scatter:
            @pl.when(i >= 2)
            def _():
                wait_scatter(slot)
        for j in range(SUBLANES):
            xb_ref[:, j * LANES:(j + 1) * LANES] = xbuf_ref[(slot,) + _lane_block(j, tm)].astype(BF16)
        gather_rows((i + 1) % IDX_RING, 1 - slot)
        if with_scatter:
            scatter_rows((i - 1) % IDX_RING, 1 - slot)
        gu = _dot(xb_ref[...], wgu_ref[0]) + bgu_ref[0]
        gate = jnp.minimum(gu[:, :D_FF], SWIGLU_LIMIT)
        up = jnp.clip(gu[:, D_FF:], -SWIGLU_LIMIT, SWIGLU_LIMIT)
        act = (up + 1.0) * (gate * jax.nn.sigmoid(SWIGLU_ALPHA * gate))
        out = _dot(act.astype(BF16), wd_ref[0]) + bd_ref[0]
        for j in range(SUBLANES):
            obuf_ref[(slot,) + _lane_block(j, tm)] = out[:, j * LANES:(j + 1) * LANES]

    @pl.when(i == 0)
    def _():
        obuf_ref[0] = jnp.zeros(obuf_ref.shape[1:], F32)
        spare = pltpu.make_async_copy(obuf_ref.at[0], out_hbm.at[pl.ds(TOP_K * n_tok * ROW_SUB, tm * ROW_SUB)],
                                      sc_sem.at[0])
        spare.start()
        spare.wait()
        cp = idx_copy(0, 0)
        cp.start()
        cp.wait()
        gather_rows(0, 0)
        idx_copy(jnp.minimum(1, last), 1).start()
        step(False)

    @pl.when(jnp.logical_and(i > 0, i < nact))
    def _():
        step(True)

    @pl.when(i == nact)
    def _():
        wait_gather(i % 2)
        idx_copy(0, (i + 1) % IDX_RING).wait()

        @pl.when(i >= 2)
        def _():
            wait_scatter(i % 2)

        scatter_rows((i - 1) % IDX_RING, (i - 1) % 2)
        wait_scatter((i - 1) % 2)


def _moe(h2, slot_asg, blk_e, nact, wgu_bf, b_gate_up, wd_bf, b_down, out_rows):
    n_blocks = blk_e.shape[0]
    n_tok = h2.shape[0] // ROW_SUB
    assert n_tok & (n_tok - 1) == 0, "token count must be a power of two"
    tm = MOE_TILE
    grid_spec = pltpu.PrefetchScalarGridSpec(
        num_scalar_prefetch=2,
        grid=(n_blocks,),
        in_specs=[pl.BlockSpec(memory_space=pl.ANY),
                  pl.BlockSpec(memory_space=pl.ANY),
                  pl.BlockSpec((1, D_MODEL, 2 * D_FF), lambda i, be, na: (be[i], 0, 0)),
                  pl.BlockSpec((1, 1, 2 * D_FF), lambda i, be, na: (be[i], 0, 0)),
                  pl.BlockSpec((1, D_FF, D_MODEL), lambda i, be, na: (be[i], 0, 0)),
                  pl.BlockSpec((1, 1, D_MODEL), lambda i, be, na: (be[i], 0, 0))],
        out_specs=pl.BlockSpec(memory_space=pl.ANY),
        scratch_shapes=[pltpu.SMEM((IDX_RING, tm), jnp.int32),
                        pltpu.VMEM((2, tm * ROW_SUB, LANES), F32),
                        pltpu.VMEM((2, tm * ROW_SUB, LANES), F32),
                        pltpu.VMEM((tm, D_MODEL), BF16),
                        pltpu.SemaphoreType.DMA((IDX_RING,)),
                        pltpu.SemaphoreType.DMA((2,)),
                        pltpu.SemaphoreType.DMA((2,))],
    )
    return pl.pallas_call(
        functools.partial(_moe_kernel, n_tok=n_tok),
        out_shape=jax.ShapeDtypeStruct((out_rows * ROW_SUB, LANES), F32),
        grid_spec=grid_spec,
        compiler_params=pltpu.CompilerParams(dimension_semantics=("arbitrary",), vmem_limit_bytes=VMEM_LIMIT,
                                             disable_bounds_checks=True),
        name="moe",
    )(blk_e, nact, slot_asg.reshape(n_blocks, tm), h2, wgu_bf, b_gate_up.reshape(N_EXPERTS, 1, -1),
      wd_bf, b_down.reshape(N_EXPERTS, 1, -1))


COMBINE_TILE = 256


def _combine_kernel(o0_ref, o1_ref, o2_ref, o3_ref, x1_ref, w_ref, mod_ref, g_ref, b_ref, o_ref):
    w = w_ref[...]
    parts = []
    for j in range(SUBLANES):
        blk = _lane_block(j, w.shape[0])
        parts.append(w[:, 0:1] * o0_ref[blk] + w[:, 1:2] * o1_ref[blk]
                     + w[:, 2:3] * o2_ref[blk] + w[:, 3:4] * o3_ref[blk])
    y = jnp.concatenate(parts, axis=1)
    m = mod_ref[0]
    o_ref[...] = _ln(DEEPNORM_ALPHA * x1_ref[...] + m[5:6] * y) * g_ref[...] + b_ref[...]


def _combine(outs, x1, top_w, mod3, ln2_g, ln2_b, seq, b0):
    t = x1.shape[0]
    tk = COMBINE_TILE
    nt = t // tk
    full = pl.BlockSpec((tk, D_MODEL), lambda i: (i, 0))
    vec = pl.BlockSpec((1, D_MODEL), lambda i: (0, 0))
    expert_rows = [pl.BlockSpec((tk * ROW_SUB, LANES), lambda i, k=k: (k * nt + i, 0)) for k in range(TOP_K)]
    return pl.pallas_call(
        _combine_kernel,
        out_shape=jax.ShapeDtypeStruct((t, D_MODEL), F32),
        grid=(nt,),
        in_specs=expert_rows + [full, pl.BlockSpec((tk, TOP_K), lambda i: (i, 0)),
                                pl.BlockSpec((1, 6, D_MODEL), lambda i: (b0 + (i * tk) // seq, 0, 0)),
                                vec, vec],
        out_specs=full,
        compiler_params=_params(("arbitrary",)),
        name="combine",
    )(outs, outs, outs, outs, x1, top_w, mod3, ln2_g.reshape(1, -1), ln2_b.reshape(1, -1))


def _gate_weights(w_rg, b_rg):
    hpb = LANES // REC_HEAD_DIM
    nblk = D_REC // LANES
    w = w_rg.reshape(2, 2, nblk, hpb, REC_HEAD_DIM, REC_HEAD_DIM)
    eye = jnp.eye(hpb, dtype=w.dtype)
    blk = jnp.einsum('dgbhij,hk->dgbhikj', w, eye).reshape(2, 2, nblk, LANES, LANES)
    wg = jnp.concatenate([blk[:, 0], blk[:, 1]], axis=-1).astype(BF16)
    bg = b_rg.reshape(2, 2, D_REC)
    return wg, bg


def _routing_tables(top_e, rank, counts, tm):
    t = top_e.shape[0]
    n_asg = t * TOP_K
    counts = counts.reshape(N_EXPERTS).astype(jnp.int32)
    padded = (counts + tm - 1) // tm * tm
    pad_ends = jnp.cumsum(padded)
    pad_starts = pad_ends - padded
    n_blocks = n_asg // tm + N_EXPERTS
    blk_start = jnp.arange(n_blocks, dtype=jnp.int32) * tm
    blk_e = jnp.minimum(jnp.sum((pad_ends[None, :] <= blk_start[:, None]).astype(jnp.int32), axis=1),
                        N_EXPERTS - 1)
    nact = (pad_ends[-1] // tm).astype(jnp.int32).reshape(1)
    pad_row = n_asg + jnp.arange(n_blocks * tm, dtype=jnp.int32) % tm
    e_km = top_e.T
    dest = pad_starts[e_km] + rank.T
    asg = jnp.arange(n_asg, dtype=jnp.int32).reshape(TOP_K, t)
    slot_asg = pad_row.at[dest.reshape(-1)].set(asg.reshape(-1), unique_indices=True,
                                                mode='promise_in_bounds')
    return slot_asg, blk_e, nact


def _trunk(x, mod3, b0, p):
    batch, seq, _ = x.shape
    t = batch * seq
    x2d = x.reshape(t, D_MODEL)
    ux, ug, zr, zi = _inproj(x2d, mod3, p['w_in_bf'], seq, b0)
    rec = _rg_lru(ux, ug, p['conv_w'], p['conv_b'], p['wg'], p['bg'], p['lam'], batch, seq)
    four = _seq_dft(zr, zi, batch, seq)
    x1, h2, top_e, top_w, rank, counts = _mix_route(rec, four, x2d, mod3, p['g_mix'], p['w_out_bf'],
                                                    p['ln1_g'], p['ln1_b'], p['w_router'], p['b_router'],
                                                    seq, b0)
    slot_asg, blk_e, nact = _routing_tables(top_e, rank, counts, MOE_TILE)
    outs = _moe(h2, slot_asg, blk_e, nact, p['wgu_bf'], p['b_gate_up'], p['wd_bf'], p['b_down'],
                out_rows=t * TOP_K + MOE_TILE)
    y = _combine(outs, x1, top_w, mod3, p['ln2_g'], p['ln2_b'], seq, b0)
    return y.reshape(batch, seq, D_MODEL)


def kernel(x_prompt, x_sample, c_prompt, c_sample, w_ada, b_ada, w_in, conv_w, conv_b, w_rg, b_rg,
           lru_lambda, g_mix, w_out, ln1_g, ln1_b, w_router, b_router, w_gate_up, b_gate_up,
           w_down, b_down, ln2_g, ln2_b):
    depth = w_ada.shape[0]
    xs = [x_prompt, x_sample]
    cs = [c_prompt, c_sample]
    nb = [c.shape[0] for c in cs]
    bp = -(-sum(nb) // SUBLANES) * SUBLANES
    for l in range(depth):
        wg, bg = _gate_weights(w_rg[l], b_rg[l])
        p = dict(w_in_bf=w_in[l].astype(BF16), conv_w=conv_w[l], conv_b=conv_b[l].reshape(1, -1),
                 wg=wg, bg=bg, lam=lru_lambda[l], g_mix=g_mix[l], w_out_bf=w_out[l].astype(BF16),
                 ln1_g=ln1_g[l], ln1_b=ln1_b[l], w_router=w_router[l], b_router=b_router[l],
                 wgu_bf=w_gate_up[l].astype(BF16), b_gate_up=b_gate_up[l],
                 wd_bf=w_down[l].astype(BF16), b_down=b_down[l], ln2_g=ln2_g[l], ln2_b=ln2_b[l])
        c_all = jnp.concatenate(cs + [jnp.zeros((bp - sum(nb), D_MODEL), F32)], axis=0)
        mod3 = _ada_mod(c_all, w_ada[l], b_ada[l]).reshape(bp, 6, D_MODEL)
        xs = [_trunk(xs[0], mod3, 0, p), _trunk(xs[1], mod3, nb[0], p)]
    return (xs[0], xs[1])
```

```python
import functools
import math

import numpy as np
import jax
import jax.numpy as jnp
from jax import lax
from jax.experimental import pallas as pl
from jax.experimental.pallas import tpu as pltpu

D_MODEL = 1024
D_REC = 512
D_FFT = 512
N_REC_HEADS = 8
REC_HEAD_DIM = 64
FFT_GROUP_DIM = 128
N_FFT_GROUPS = 4
CONV_WIDTH = 4
CONV_LEFT = 2
LRU_C = 8.0
N_EXPERTS = 32
TOP_K = 4
D_FF = 1024
SWIGLU_LIMIT = 7.0
SWIGLU_ALPHA = 1.702
DEEPNORM_ALPHA = 2.0 ** 0.25
LN_EPS = 1e-5

LANES = 128
SUBLANES = 8
ROW_SUB = D_MODEL // LANES
assert ROW_SUB == SUBLANES
DFT_N2 = 128
VMEM_LIMIT = 48 * 1024 * 1024

F32 = jnp.float32
BF16 = jnp.bfloat16


def _lane_block(j, n):
    return (pl.ds(j, n, stride=ROW_SUB), slice(None))


def _params(sem, vmem=VMEM_LIMIT):
    return pltpu.CompilerParams(dimension_semantics=sem, vmem_limit_bytes=vmem)


def _dot(a, b):
    return jnp.dot(a, b, preferred_element_type=F32)


def _dot3(a, b):
    a_hi = a.astype(BF16)
    b_hi = b.astype(BF16)
    a_lo = (a - a_hi.astype(F32)).astype(BF16)
    b_lo = (b - b_hi.astype(F32)).astype(BF16)
    return _dot(a_hi, b_hi) + _dot(a_lo, b_hi) + _dot(a_hi, b_lo)


def _ln(x):
    mu = jnp.mean(x, axis=-1, keepdims=True)
    xc = x - mu
    var = jnp.mean(xc * xc, axis=-1, keepdims=True)
    return xc * lax.rsqrt(var + LN_EPS)


def _ada_kernel(c_ref, w_ref, b_ref, o_ref):
    c = c_ref[...]
    s = c * jax.nn.sigmoid(c)
    o_ref[...] = _dot3(s, w_ref[...]) + b_ref[...]


def _ada_mod(c_all, w_ada, b_ada):
    bp = c_all.shape[0]
    n = w_ada.shape[1]
    tn = 768
    return pl.pallas_call(
        _ada_kernel,
        out_shape=jax.ShapeDtypeStruct((bp, n), F32),
        grid=(n // tn,),
        in_specs=[pl.BlockSpec((bp, D_MODEL), lambda j: (0, 0)),
                  pl.BlockSpec((D_MODEL, tn), lambda j: (0, j)),
                  pl.BlockSpec((1, tn), lambda j: (0, j))],
        out_specs=pl.BlockSpec((bp, tn), lambda j: (0, j)),
        compiler_params=_params(("arbitrary",)),
        name="ada_mod",
    )(c_all, w_ada, b_ada.reshape(1, n))


def _inproj_kernel(x_ref, mod_ref, w_ref, dft_ref, ux_ref, ug_ref, zr_ref, zi_ref):
    m = mod_ref[0]
    h = _ln(x_ref[...]) * (1.0 + m[1:2]) + m[0:1]
    u = _dot(h.astype(BF16), w_ref[...])
    ux_ref[...] = u[:, :D_REC]
    ug_ref[...] = u[:, D_REC:2 * D_REC]
    for g in range(N_FFT_GROUPS):
        lo = 2 * D_REC + g * FFT_GROUP_DIM
        z = _dot(u[:, lo:lo + FFT_GROUP_DIM].astype(BF16), dft_ref[...])
        zr_ref[:, g * FFT_GROUP_DIM:(g + 1) * FFT_GROUP_DIM] = z[:, :FFT_GROUP_DIM]
        zi_ref[:, g * FFT_GROUP_DIM:(g + 1) * FFT_GROUP_DIM] = z[:, FFT_GROUP_DIM:]


def _group_dft_matrix():
    j = np.arange(FFT_GROUP_DIM)
    ang = 2.0 * np.pi * ((j[:, None] * j[None, :]) % FFT_GROUP_DIM) / FFT_GROUP_DIM
    return jnp.asarray(np.concatenate([np.cos(ang), -np.sin(ang)], axis=1), dtype=F32).astype(BF16)


def _inproj(x2d, mod3, w_in_bf, seq, b0):
    t = x2d.shape[0]
    tm = 512
    row = pl.BlockSpec((tm, D_REC), lambda i: (i, 0))
    return pl.pallas_call(
        _inproj_kernel,
        out_shape=[jax.ShapeDtypeStruct((t, D_REC), F32)] * 4,
        grid=(t // tm,),
        in_specs=[pl.BlockSpec((tm, D_MODEL), lambda i: (i, 0)),
                  pl.BlockSpec((1, 6, D_MODEL), lambda i: (b0 + (i * tm) // seq, 0, 0)),
                  pl.BlockSpec((D_MODEL, 3 * D_REC), lambda i: (0, 0)),
                  pl.BlockSpec((FFT_GROUP_DIM, 2 * FFT_GROUP_DIM), lambda i: (0, 0))],
        out_specs=[row, row, row, row],
        compiler_params=_params(("arbitrary",)),
        name="inproj",
    )(x2d, mod3, w_in_bf, _group_dft_matrix())


REC_CHUNK = 512


def _scan_chunk(a, b, carry, h_ref, reverse):
    rows, width = a.shape
    n_tiles = rows // SUBLANES
    a = a.reshape(n_tiles, SUBLANES, width)
    b = b.reshape(n_tiles, SUBLANES, width)
    ridx = lax.broadcasted_iota(jnp.int32, a.shape, 1)
    for s in (1, 2, 4):
        shift = SUBLANES - s if reverse else s
        keep = ridx < SUBLANES - s if reverse else ridx >= s
        a_sh = pltpu.roll(a, shift, 1)
        b_sh = pltpu.roll(b, shift, 1)
        b = jnp.where(keep, a * b_sh + b, b)
        a = jnp.where(keep, a * a_sh, a)
    order = range(n_tiles - 1, -1, -1) if reverse else range(n_tiles)
    for j in order:
        h = b[j] + a[j] * carry
        carry = h[0:1] if reverse else h[SUBLANES - 1:SUBLANES]
        h_ref[j * SUBLANES:(j + 1) * SUBLANES, :] = h
    return carry


def _conv_chunk(prev_ref, x_ref, next_ref, cw_ref, cb_ref, first, last):
    rows = x_ref.shape[0]
    prev = jnp.where(first, 0.0, prev_ref[...])
    nxt = jnp.where(last, 0.0, next_ref[...])
    ext = jnp.concatenate([prev, x_ref[...], nxt], axis=0)
    n = rows + 2 * SUBLANES
    body = slice(SUBLANES, SUBLANES + rows)
    cw = cw_ref[...]
    xc = cw[2:3] * ext[body]
    xc = xc + cw[0:1] * pltpu.roll(ext, 2, 0)[body]
    xc = xc + cw[1:2] * pltpu.roll(ext, 1, 0)[body]
    xc = xc + cw[3:4] * pltpu.roll(ext, n - 1, 0)[body]
    return xc + cb_ref[...]


def _sigmoid(x):
    return 0.5 * jnp.tanh(0.5 * x) + 0.5


def _neg_expm1_of_double(half_y):
    t = jnp.tanh(half_y)
    return (-2.0 * t) / (1.0 - t)


def _lru_coeffs(xc, wg_ref, bg_ref, lam_ref):
    lam = lam_ref[...]
    neg = -lam
    softplus = jnp.maximum(neg, 0.0) + jnp.log(1.0 + jnp.exp(-jnp.abs(neg)))
    xb = xc.astype(BF16)
    r_parts, i_parts = [], []
    for cb in range(D_REC // LANES):
        g = _dot(xb[:, cb * LANES:(cb + 1) * LANES], wg_ref[cb])
        r_parts.append(g[:, :LANES])
        i_parts.append(g[:, LANES:])
    bg = bg_ref[...]
    r = _sigmoid(jnp.concatenate(r_parts, axis=1) + bg[0:1])
    i = _sigmoid(jnp.concatenate(i_parts, axis=1) + bg[1:2])
    log_a = (-LRU_C) * r * softplus
    a = jnp.exp(log_a)
    mult = jnp.sqrt(_neg_expm1_of_double(log_a))
    return a, mult * (i * xc)


def _rec_fwd_kernel(prev_ref, x_ref, next_ref, cw_ref, cb_ref, wg_ref, bg_ref, lam_ref,
                    hf_ref, carry_ref):
    c = pl.program_id(1)
    nch = pl.num_programs(1)

    @pl.when(c == 0)
    def _():
        carry_ref[...] = jnp.zeros_like(carry_ref)

    xc = _conv_chunk(prev_ref, x_ref, next_ref, cw_ref, cb_ref, c == 0, c == nch - 1)
    a, b = _lru_coeffs(xc, wg_ref, bg_ref, lam_ref)
    carry_ref[...] = _scan_chunk(a, b, carry_ref[...], hf_ref, reverse=False)


def _rec_bwd_kernel(prev_ref, x_ref, next_ref, cw_ref, cb_ref, wg_ref, bg_ref, lam_ref,
                    hf_ref, ug_ref, rec_ref, carry_ref, hb_ref):
    j = pl.program_id(1)
    nch = pl.num_programs(1)

    @pl.when(j == 0)
    def _():
        carry_ref[...] = jnp.zeros_like(carry_ref)

    xc = _conv_chunk(prev_ref, x_ref, next_ref, cw_ref, cb_ref, j == nch - 1, j == 0)
    a, b = _lru_coeffs(xc, wg_ref, bg_ref, lam_ref)
    carry_ref[...] = _scan_chunk(a, b, carry_ref[...], hb_ref, reverse=True)
    ug = ug_ref[...]
    gelu = 0.5 * ug * (1.0 + jnp.tanh(math.sqrt(2.0 / math.pi) * (ug + 0.044715 * (ug * ug * ug))))
    rec_ref[...] = (hf_ref[...] + hb_ref[...]) * gelu


def _rec_specs(batch, seq, reverse):
    lc = min(REC_CHUNK, seq)
    nch = seq // lc
    tiles = lc // SUBLANES
    last_tile = batch * nch * tiles - 1

    def chunk(b, j):
        return nch - 1 - j if reverse else j

    row = pl.BlockSpec((lc, D_REC), lambda b, j: (b * nch + chunk(b, j), 0))
    prev = pl.BlockSpec((SUBLANES, D_REC),
                        lambda b, j: (jnp.maximum((b * nch + chunk(b, j)) * tiles - 1, 0), 0))
    nxt = pl.BlockSpec((SUBLANES, D_REC),
                       lambda b, j: (jnp.minimum((b * nch + chunk(b, j) + 1) * tiles, last_tile), 0))
    const2 = lambda shape: pl.BlockSpec(shape, lambda b, j: (0,) * len(shape))
    weights = [const2((CONV_WIDTH, D_REC)), const2((1, D_REC)),
               const2((D_REC // LANES, LANES, 2 * LANES)), const2((2, D_REC)), const2((1, D_REC))]
    return row, prev, nxt, weights, nch, lc


def _rg_lru(ux, ug, conv_w, conv_b, wg, bg, lam, batch, seq):
    t = ux.shape[0]
    row, prev, nxt, wspecs, nch, lc = _rec_specs(batch, seq, False)
    hf = pl.pallas_call(
        _rec_fwd_kernel,
        out_shape=jax.ShapeDtypeStruct((t, D_REC), F32),
        grid=(batch, nch),
        in_specs=[prev, row, nxt] + wspecs,
        out_specs=row,
        scratch_shapes=[pltpu.VMEM((1, D_REC), F32)],
        compiler_params=_params(("arbitrary", "arbitrary")),
        name="rec_fwd",
    )(ux, ux, ux, conv_w, conv_b, wg[0], bg[0], lam[0:1])
    row, prev, nxt, wspecs, nch, lc = _rec_specs(batch, seq, True)
    return pl.pallas_call(
        _rec_bwd_kernel,
        out_shape=jax.ShapeDtypeStruct((t, D_REC), F32),
        grid=(batch, nch),
        in_specs=[prev, row, nxt] + wspecs + [row, row],
        out_specs=row,
        scratch_shapes=[pltpu.VMEM((1, D_REC), F32), pltpu.VMEM((lc, D_REC), F32)],
        compiler_params=_params(("arbitrary", "arbitrary")),
        name="rec_bwd",
    )(ux, ux, ux, conv_w, conv_b, wg[1], bg[1], lam[1:2], hf, ug)


DFT_A_JB = 8
DFT_C_GB = 2


def _dft_a_kernel(zr_ref, zi_ref, t_ref, a_ref):
    for jb in range(zr_ref.shape[2]):
        for r in range(SUBLANES):
            rhs = jnp.concatenate([zr_ref[0, :, jb, r, :], zi_ref[0, :, jb, r, :]], axis=0)
            a_ref[0, 0, :, jb, r, :] = _dot(t_ref[jb * SUBLANES + r], rhs.astype(BF16))


def _dft_c_kernel(a_ref, m_ref, o_ref, *, scale):
    for g in range(a_ref.shape[3]):
        for j in range(SUBLANES):
            rhs = jnp.concatenate([a_ref[0, 0, 0, g, j], a_ref[0, 0, 1, g, j]], axis=0)
            o_ref[0, :, g, j, :] = _dot(m_ref[...], rhs.astype(BF16)) * scale


def _dft_tables(seq):
    n2 = DFT_N2
    n1 = seq // n2
    k1 = jnp.arange(n1, dtype=jnp.int32)
    s = n2 * jnp.arange(n1, dtype=jnp.int32)[None, None, :] + jnp.arange(n2, dtype=jnp.int32)[:, None, None]
    ang = (2.0 * math.pi / seq) * ((k1[None, :, None] * s) % seq).astype(F32)
    cs, sn = jnp.cos(ang), jnp.sin(ang)
    t_a = jnp.concatenate([jnp.concatenate([cs, sn], axis=2),
                           jnp.concatenate([-sn, cs], axis=2)], axis=1).astype(BF16)
    k2 = jnp.arange(n2, dtype=jnp.int32)
    phi = (2.0 * math.pi / n2) * ((k2[:, None] * k2[None, :]) % n2).astype(F32)
    m_c = jnp.concatenate([jnp.cos(phi), jnp.sin(phi)], axis=1).astype(BF16)
    return t_a, m_c


def _seq_dft(zr, zi, batch, seq):
    n2 = DFT_N2
    n1 = seq // n2
    nj = n2 // SUBLANES
    ncb = D_FFT // LANES
    t_a, m_c = _dft_tables(seq)
    z5 = (batch, n1, nj, SUBLANES, D_FFT)
    jb = min(DFT_A_JB, nj)
    zspec = pl.BlockSpec((1, n1, jb, SUBLANES, LANES), lambda b, cb, j: (b, 0, j, 0, cb))
    a6 = pl.pallas_call(
        _dft_a_kernel,
        out_shape=jax.ShapeDtypeStruct((batch, ncb, 2 * n1, nj, SUBLANES, LANES), F32),
        grid=(batch, ncb, nj // jb),
        in_specs=[zspec, zspec,
                  pl.BlockSpec((jb * SUBLANES, 2 * n1, 2 * n1), lambda b, cb, j: (j, 0, 0))],
        out_specs=pl.BlockSpec((1, 1, 2 * n1, jb, SUBLANES, LANES), lambda b, cb, j: (b, cb, 0, j, 0, 0)),
        compiler_params=_params(("arbitrary",) * 3),
        name="dft_a",
    )(zr.reshape(z5), zi.reshape(z5), t_a)
    ng = n1 // SUBLANES
    gb = min(DFT_C_GB, ng)
    a6 = a6.reshape(batch, ncb, 2, ng, SUBLANES, n2, LANES)
    out = pl.pallas_call(
        functools.partial(_dft_c_kernel, scale=1.0 / math.sqrt(seq * FFT_GROUP_DIM)),
        out_shape=jax.ShapeDtypeStruct((batch, n2, ng, SUBLANES, D_FFT), F32),
        grid=(batch, ncb, ng // gb),
        in_specs=[pl.BlockSpec((1, 1, 2, gb, SUBLANES, n2, LANES), lambda b, cb, g: (b, cb, 0, g, 0, 0, 0)),
                  pl.BlockSpec((n2, 2 * n2), lambda b, cb, g: (0, 0))],
        out_specs=pl.BlockSpec((1, n2, gb, SUBLANES, LANES), lambda b, cb, g: (b, 0, g, 0, cb)),
        compiler_params=_params(("arbitrary",) * 3),
        name="dft_c",
    )(a6, m_c)
    return out.reshape(batch * seq, D_FFT)


ROUTE_TILE = 512


def _mix_route_kernel(rec_ref, four_ref, x_ref, mod_ref, gmix_ref, wout_ref, ln1g_ref, ln1b_ref,
                      wr_ref, br_ref, tri_ref,
                      x1_ref, h2_ref, tope_ref, topw_ref, rank_ref, cnt_ref, run_ref):
    i = pl.program_id(0)

    @pl.when(i == 0)
    def _():
        run_ref[...] = jnp.zeros_like(run_ref)

    m = mod_ref[0]
    gm = gmix_ref[...]
    rec = rec_ref[...]
    four = four_ref[...]
    rn = rec * lax.rsqrt(jnp.mean(rec * rec, axis=-1, keepdims=True) + LN_EPS) * gm[:, :D_REC]
    fn = four * lax.rsqrt(jnp.mean(four * four, axis=-1, keepdims=True) + LN_EPS) * gm[:, D_REC:]
    mix = _dot(rn.astype(BF16), wout_ref[:D_REC, :]) + _dot(fn.astype(BF16), wout_ref[D_REC:, :])
    x1 = _ln(DEEPNORM_ALPHA * x_ref[...] + m[2:3] * mix) * ln1g_ref[...] + ln1b_ref[...]
    x1_ref[...] = x1
    h2 = _ln(x1) * (1.0 + m[4:5]) + m[3:4]
    for j in range(SUBLANES):
        h2_ref[_lane_block(j, h2.shape[0])] = h2[:, j * LANES:(j + 1) * LANES]
    logits = _dot3(h2, wr_ref[...]) + br_ref[...]
    lane = lax.broadcasted_iota(jnp.int32, logits.shape, 1).astype(F32)
    work = logits
    onehot = jnp.zeros(logits.shape, F32)
    vals, idxs = [], []
    for _ in range(TOP_K):
        v = jnp.max(work, axis=-1, keepdims=True)
        e = jnp.min(jnp.where(work == v, lane, float(N_EXPERTS)), axis=-1, keepdims=True)
        sel = lane == e
        onehot = jnp.where(sel, 1.0, onehot)
        work = jnp.where(sel, -jnp.inf, work)
        vals.append(v)
        idxs.append(e)
    ex = [jnp.exp(v - vals[0]) for v in vals]
    denom = ex[0] + ex[1] + ex[2] + ex[3]
    before = _dot(tri_ref[...], onehot.astype(BF16)) + run_ref[...]
    for k in range(TOP_K):
        tope_ref[:, k:k + 1] = idxs[k].astype(jnp.int32)
        topw_ref[:, k:k + 1] = ex[k] / denom
        rank_ref[:, k:k + 1] = jnp.sum(jnp.where(lane == idxs[k], before, 0.0), axis=-1,
                                       keepdims=True).astype(jnp.int32)
    run_ref[...] = run_ref[...] + jnp.sum(onehot, axis=0, keepdims=True)
    cnt_ref[...] = run_ref[...]


def _mix_route(rec, four, x2d, mod3, g_mix, w_out_bf, ln1_g, ln1_b, w_router, b_router, seq, b0):
    t = x2d.shape[0]
    tm = ROUTE_TILE
    tri = jnp.asarray(np.tril(np.ones((tm, tm), np.float32), -1), dtype=BF16)
    half = pl.BlockSpec((tm, D_REC), lambda i: (i, 0))
    full = pl.BlockSpec((tm, D_MODEL), lambda i: (i, 0))
    vec = pl.BlockSpec((1, D_MODEL), lambda i: (0, 0))
    k4 = pl.BlockSpec((tm, TOP_K), lambda i: (i, 0))
    return pl.pallas_call(
        _mix_route_kernel,
        out_shape=[jax.ShapeDtypeStruct((t, D_MODEL), F32), jax.ShapeDtypeStruct((t * ROW_SUB, LANES), F32),
                   jax.ShapeDtypeStruct((t, TOP_K), jnp.int32), jax.ShapeDtypeStruct((t, TOP_K), F32),
                   jax.ShapeDtypeStruct((t, TOP_K), jnp.int32), jax.ShapeDtypeStruct((1, N_EXPERTS), F32)],
        grid=(t // tm,),
        in_specs=[half, half, full,
                  pl.BlockSpec((1, 6, D_MODEL), lambda i: (b0 + (i * tm) // seq, 0, 0)),
                  vec, pl.BlockSpec((D_MODEL, D_MODEL), lambda i: (0, 0)), vec, vec,
                  pl.BlockSpec((D_MODEL, N_EXPERTS), lambda i: (0, 0)),
                  pl.BlockSpec((1, N_EXPERTS), lambda i: (0, 0)),
                  pl.BlockSpec((tm, tm), lambda i: (0, 0))],
        out_specs=[full, pl.BlockSpec((tm * ROW_SUB, LANES), lambda i: (i, 0)), k4, k4, k4,
                   pl.BlockSpec((1, N_EXPERTS), lambda i: (0, 0))],
        scratch_shapes=[pltpu.VMEM((1, N_EXPERTS), F32)],
        compiler_params=_params(("arbitrary",)),
        name="mix_route",
    )(rec, four, x2d, mod3, g_mix.reshape(1, -1), w_out_bf, ln1_g.reshape(1, -1), ln1_b.reshape(1, -1),
      w_router, b_router.reshape(1, -1), tri)


MOE_TILE = 256
IDX_RING = 4


def _moe_kernel(blk_e_ref, nact_ref, asg_hbm, h_hbm, wgu_ref, bgu_ref, wd_ref, bd_ref, out_hbm,
                idx_ref, xbuf_ref, obuf_ref, xb_ref, idx_sem, row_sem, sc_sem, *, n_tok):
    i = pl.program_id(0)
    nact = nact_ref[0]
    tm = MOE_TILE
    last = nact - 1

    def idx_copy(blk, ring):
        return pltpu.make_async_copy(asg_hbm.at[blk], idx_ref.at[ring], idx_sem.at[ring])

    def gather_rows(ring, slot):
        for r in range(tm):
            src = pl.multiple_of((idx_ref[ring, r] & (n_tok - 1)) * ROW_SUB, ROW_SUB)
            pltpu.make_async_copy(h_hbm.at[pl.ds(src, ROW_SUB)], xbuf_ref.at[slot, pl.ds(r * ROW_SUB, ROW_SUB)],
                                  row_sem.at[slot]).start(priority=r % 2)

    def scatter_rows(ring, slot):
        for r in range(tm):
            row = pl.multiple_of(idx_ref[ring, r] * ROW_SUB, ROW_SUB)
            pltpu.make_async_copy(obuf_ref.at[slot, pl.ds(r * ROW_SUB, ROW_SUB)], out_hbm.at[pl.ds(row, ROW_SUB)],
                                  sc_sem.at[slot]).start(priority=r % 2)

    def wait_gather(slot):
        pltpu.make_async_copy(h_hbm.at[pl.ds(0, tm * ROW_SUB)], xbuf_ref.at[slot], row_sem.at[slot]).wait()

    def wait_scatter(slot):
        pltpu.make_async_copy(obuf_ref.at[slot], out_hbm.at[pl.ds(0, tm * ROW_SUB)], sc_sem.at[slot]).wait()

    def step(with_scatter):
        slot = i % 2
        wait_gather(slot)
        idx_copy(0, (i + 1) % IDX_RING).wait()
        idx_copy(jnp.minimum(i + 2, last), (i + 2) % IDX_RING).start()
        if with_scatter:
            @pl.when(i >= 2)
            def _():
                wait_scatter(slot)
        for j in range(SUBLANES):
            xb_ref[:, j * LANES:(j + 1) * LANES] = xbuf_ref[(slot,) + _lane_block(j, tm)].astype(BF16)
        gather_rows((i + 1) % IDX_RING, 1 - slot)
        if with_scatter:
            scatter_rows((i - 1) % IDX_RING, 1 - slot)
        gu = _dot(xb_ref[...], wgu_ref[0]) + bgu_ref[0]
        gate = jnp.minimum(gu[:, :D_FF], SWIGLU_LIMIT)
        up = jnp.clip(gu[:, D_FF:], -SWIGLU_LIMIT, SWIGLU_LIMIT)
        act = (up + 1.0) * (gate * jax.nn.sigmoid(SWIGLU_ALPHA * gate))
        out = _dot(act.astype(BF16), wd_ref[0]) + bd_ref[0]
        for j in range(SUBLANES):
            obuf_ref[(slot,) + _lane_block(j, tm)] = out[:, j * LANES:(j + 1) * LANES]

    @pl.when(i == 0)
    def _():
        obuf_ref[0] = jnp.zeros(obuf_ref.shape[1:], F32)
        spare = pltpu.make_async_copy(obuf_ref.at[0], out_hbm.at[pl.ds(TOP_K * n_tok * ROW_SUB, tm * ROW_SUB)],
                                      sc_sem.at[0])
        spare.start()
        spare.wait()
        cp = idx_copy(0, 0)
        cp.start()
        cp.wait()
        gather_rows(0, 0)
        idx_copy(jnp.minimum(1, last), 1).start()
        step(False)

    @pl.when(jnp.logical_and(i > 0, i < nact))
    def _():
        step(True)

    @pl.when(i == nact)
    def _():
        wait_gather(i % 2)
        idx_copy(0, (i + 1) % IDX_RING).wait()

        @pl.when(i >= 2)
        def _():
            wait_scatter(i % 2)

        scatter_rows((i - 1) % IDX_RING, (i - 1) % 2)
        wait_scatter((i - 1) % 2)


def _moe(h2, slot_asg, blk_e, nact, wgu_bf, b_gate_up, wd_bf, b_down, out_rows):
    n_blocks = blk_e.shape[0]
    n_tok = h2.shape[0] // ROW_SUB
    assert n_tok & (n_tok - 1) == 0, "token count must be a power of two"
    tm = MOE_TILE
    grid_spec = pltpu.PrefetchScalarGridSpec(
        num_scalar_prefetch=2,
        grid=(n_blocks,),
        in_specs=[pl.BlockSpec(memory_space=pl.ANY),
                  pl.BlockSpec(memory_space=pl.ANY),
                  pl.BlockSpec((1, D_MODEL, 2 * D_FF), lambda i, be, na: (be[i], 0, 0)),
                  pl.BlockSpec((1, 1, 2 * D_FF), lambda i, be, na: (be[i], 0, 0)),
                  pl.BlockSpec((1, D_FF, D_MODEL), lambda i, be, na: (be[i], 0, 0)),
                  pl.BlockSpec((1, 1, D_MODEL), lambda i, be, na: (be[i], 0, 0))],
        out_specs=pl.BlockSpec(memory_space=pl.ANY),
        scratch_shapes=[pltpu.SMEM((IDX_RING, tm), jnp.int32),
                        pltpu.VMEM((2, tm * ROW_SUB, LANES), F32),
                        pltpu.VMEM((2, tm * ROW_SUB, LANES), F32),
                        pltpu.VMEM((tm, D_MODEL), BF16),
                        pltpu.SemaphoreType.DMA((IDX_RING,)),
                        pltpu.SemaphoreType.DMA((2,)),
                        pltpu.SemaphoreType.DMA((2,))],
    )
    return pl.pallas_call(
        functools.partial(_moe_kernel, n_tok=n_tok),
        out_shape=jax.ShapeDtypeStruct((out_rows * ROW_SUB, LANES), F32),
        grid_spec=grid_spec,
        compiler_params=pltpu.CompilerParams(dimension_semantics=("arbitrary",), vmem_limit_bytes=VMEM_LIMIT,
                                             disable_bounds_checks=True),
        name="moe",
    )(blk_e, nact, slot_asg.reshape(n_blocks, tm), h2, wgu_bf, b_gate_up.reshape(N_EXPERTS, 1, -1),
      wd_bf, b_down.reshape(N_EXPERTS, 1, -1))


COMBINE_TILE = 512


def _combine_kernel(o0_ref, o1_ref, o2_ref, o3_ref, x1_ref, w_ref, mod_ref, g_ref, b_ref, o_ref):
    w = w_ref[...]
    parts = []
    for j in range(SUBLANES):
        blk = _lane_block(j, w.shape[0])
        parts.append(w[:, 0:1] * o0_ref[blk] + w[:, 1:2] * o1_ref[blk]
                     + w[:, 2:3] * o2_ref[blk] + w[:, 3:4] * o3_ref[blk])
    y = jnp.concatenate(parts, axis=1)
    m = mod_ref[0]
    o_ref[...] = _ln(DEEPNORM_ALPHA * x1_ref[...] + m[5:6] * y) * g_ref[...] + b_ref[...]


def _combine(outs, x1, top_w, mod3, ln2_g, ln2_b, seq, b0):
    t = x1.shape[0]
    tk = COMBINE_TILE
    nt = t // tk
    full = pl.BlockSpec((tk, D_MODEL), lambda i: (i, 0))
    vec = pl.BlockSpec((1, D_MODEL), lambda i: (0, 0))
    expert_rows = [pl.BlockSpec((tk * ROW_SUB, LANES), lambda i, k=k: (k * nt + i, 0)) for k in range(TOP_K)]
    return pl.pallas_call(
        _combine_kernel,
        out_shape=jax.ShapeDtypeStruct((t, D_MODEL), F32),
        grid=(nt,),
        in_specs=expert_rows + [full, pl.BlockSpec((tk, TOP_K), lambda i: (i, 0)),
                                pl.BlockSpec((1, 6, D_MODEL), lambda i: (b0 + (i * tk) // seq, 0, 0)),
                                vec, vec],
        out_specs=full,
        compiler_params=_params(("arbitrary",)),
        name="combine",
    )(outs, outs, outs, outs, x1, top_w, mod3, ln2_g.reshape(1, -1), ln2_b.reshape(1, -1))


def _gate_weights(w_rg, b_rg):
    hpb = LANES // REC_HEAD_DIM
    nblk = D_REC // LANES
    w = w_rg.reshape(2, 2, nblk, hpb, REC_HEAD_DIM, REC_HEAD_DIM)
    eye = jnp.eye(hpb, dtype=w.dtype)
    blk = jnp.einsum('dgbhij,hk->dgbhikj', w, eye).reshape(2, 2, nblk, LANES, LANES)
    wg = jnp.concatenate([blk[:, 0], blk[:, 1]], axis=-1).astype(BF16)
    bg = b_rg.reshape(2, 2, D_REC)
    return wg, bg


def _routing_tables(top_e, rank, counts, tm):
    t = top_e.shape[0]
    n_asg = t * TOP_K
    counts = counts.reshape(N_EXPERTS).astype(jnp.int32)
    padded = (counts + tm - 1) // tm * tm
    pad_ends = jnp.cumsum(padded)
    pad_starts = pad_ends - padded
    n_blocks = n_asg // tm + N_EXPERTS
    blk_start = jnp.arange(n_blocks, dtype=jnp.int32) * tm
    blk_e = jnp.minimum(jnp.sum((pad_ends[None, :] <= blk_start[:, None]).astype(jnp.int32), axis=1),
                        N_EXPERTS - 1)
    nact = (pad_ends[-1] // tm).astype(jnp.int32).reshape(1)
    pad_row = n_asg + jnp.arange(n_blocks * tm, dtype=jnp.int32) % tm
    e_km = top_e.T
    hit = e_km[None] == jnp.arange(N_EXPERTS, dtype=jnp.int32)[:, None, None]
    dest = rank.T + jnp.sum(jnp.where(hit, pad_starts[:, None, None], 0), axis=0)
    asg = jnp.arange(n_asg, dtype=jnp.int32).reshape(TOP_K, t)
    slot_asg = pad_row.at[dest.reshape(-1)].set(asg.reshape(-1), unique_indices=True,
                                                mode='promise_in_bounds')
    return slot_asg, blk_e, nact


def _trunk(x, mod3, b0, p):
    batch, seq, _ = x.shape
    t = batch * seq
    x2d = x.reshape(t, D_MODEL)
    ux, ug, zr, zi = _inproj(x2d, mod3, p['w_in_bf'], seq, b0)
    rec = _rg_lru(ux, ug, p['conv_w'], p['conv_b'], p['wg'], p['bg'], p['lam'], batch, seq)
    four = _seq_dft(zr, zi, batch, seq)
    x1, h2, top_e, top_w, rank, counts = _mix_route(rec, four, x2d, mod3, p['g_mix'], p['w_out_bf'],
                                                    p['ln1_g'], p['ln1_b'], p['w_router'], p['b_router'],
                                                    seq, b0)
    slot_asg, blk_e, nact = _routing_tables(top_e, rank, counts, MOE_TILE)
    outs = _moe(h2, slot_asg, blk_e, nact, p['wgu_bf'], p['b_gate_up'], p['wd_bf'], p['b_down'],
                out_rows=t * TOP_K + MOE_TILE)
    y = _combine(outs, x1, top_w, mod3, p['ln2_g'], p['ln2_b'], seq, b0)
    return y.reshape(batch, seq, D_MODEL)


def kernel(x_prompt, x_sample, c_prompt, c_sample, w_ada, b_ada, w_in, conv_w, conv_b, w_rg, b_rg,
           lru_lambda, g_mix, w_out, ln1_g, ln1_b, w_router, b_router, w_gate_up, b_gate_up,
           w_down, b_down, ln2_g, ln2_b):
    depth = w_ada.shape[0]
    xs = [x_prompt, x_sample]
    cs = [c_prompt, c_sample]
    nb = [c.shape[0] for c in cs]
    bp = -(-sum(nb) // SUBLANES) * SUBLANES
    for l in range(depth):
        wg, bg = _gate_weights(w_rg[l], b_rg[l])
        p = dict(w_in_bf=w_in[l].astype(BF16), conv_w=conv_w[l], conv_b=conv_b[l].reshape(1, -1),
                 wg=wg, bg=bg, lam=lru_lambda[l], g_mix=g_mix[l], w_out_bf=w_out[l].astype(BF16),
                 ln1_g=ln1_g[l], ln1_b=ln1_b[l], w_router=w_router[l], b_router=b_router[l],
                 wgu_bf=w_gate_up[l].astype(BF16), b_gate_up=b_gate_up[l],
                 wd_bf=w_down[l].astype(BF16), b_down=b_down[l], ln2_g=ln2_g[l], ln2_b=ln2_b[l])
        c_all = jnp.concatenate(cs + [jnp.zeros((bp - sum(nb), D_MODEL), F32)], axis=0)
        mod3 = _ada_mod(c_all, w_ada[l], b_ada[l]).reshape(bp, 6, D_MODEL)
        xs = [_trunk(xs[0], mod3, 0, p), _trunk(xs[1], mod3, nb[0], p)]
    return (xs[0], xs[1])
```

```python
import functools
import math

import numpy as np
import jax
import jax.numpy as jnp
from jax import lax
from jax.experimental import pallas as pl
from jax.experimental.pallas import tpu as pltpu

D_MODEL = 1024
D_REC = 512
D_FFT = 512
N_REC_HEADS = 8
REC_HEAD_DIM = 64
FFT_GROUP_DIM = 128
N_FFT_GROUPS = 4
CONV_WIDTH = 4
CONV_LEFT = 2
LRU_C = 8.0
N_EXPERTS = 32
TOP_K = 4
D_FF = 1024
SWIGLU_LIMIT = 7.0
SWIGLU_ALPHA = 1.702
DEEPNORM_ALPHA = 2.0 ** 0.25
LN_EPS = 1e-5

LANES = 128
SUBLANES = 8
ROW_SUB = D_MODEL // LANES
assert ROW_SUB == SUBLANES
DFT_N2 = 128
VMEM_LIMIT = 48 * 1024 * 1024

F32 = jnp.float32
BF16 = jnp.bfloat16


def _lane_block(j, n):
    return (pl.ds(j, n, stride=ROW_SUB), slice(None))


def _params(sem, vmem=VMEM_LIMIT):
    return pltpu.CompilerParams(dimension_semantics=sem, vmem_limit_bytes=vmem)


def _dot(a, b):
    return jnp.dot(a, b, preferred_element_type=F32)


def _dot3(a, b):
    a_hi = a.astype(BF16)
    b_hi = b.astype(BF16)
    a_lo = (a - a_hi.astype(F32)).astype(BF16)
    b_lo = (b - b_hi.astype(F32)).astype(BF16)
    return _dot(a_hi, b_hi) + _dot(a_lo, b_hi) + _dot(a_hi, b_lo)


def _ln(x):
    mu = jnp.mean(x, axis=-1, keepdims=True)
    xc = x - mu
    var = jnp.mean(xc * xc, axis=-1, keepdims=True)
    return xc * lax.rsqrt(var + LN_EPS)


def _ada_kernel(c_ref, w_ref, b_ref, o_ref):
    c = c_ref[...]
    s = c * jax.nn.sigmoid(c)
    o_ref[...] = _dot3(s, w_ref[...]) + b_ref[...]


def _ada_mod(c_all, w_ada, b_ada):
    bp = c_all.shape[0]
    n = w_ada.shape[1]
    tn = 768
    return pl.pallas_call(
        _ada_kernel,
        out_shape=jax.ShapeDtypeStruct((bp, n), F32),
        grid=(n // tn,),
        in_specs=[pl.BlockSpec((bp, D_MODEL), lambda j: (0, 0)),
                  pl.BlockSpec((D_MODEL, tn), lambda j: (0, j)),
                  pl.BlockSpec((1, tn), lambda j: (0, j))],
        out_specs=pl.BlockSpec((bp, tn), lambda j: (0, j)),
        compiler_params=_params(("arbitrary",)),
        name="ada_mod",
    )(c_all, w_ada, b_ada.reshape(1, n))


def _inproj_kernel(x_ref, mod_ref, w_ref, dft_ref, ux_ref, ug_ref, zr_ref, zi_ref):
    m = mod_ref[0]
    h = _ln(x_ref[...]) * (1.0 + m[1:2]) + m[0:1]
    u = _dot(h.astype(BF16), w_ref[...])
    ux_ref[...] = u[:, :D_REC]
    ug_ref[...] = u[:, D_REC:2 * D_REC]
    for g in range(N_FFT_GROUPS):
        lo = 2 * D_REC + g * FFT_GROUP_DIM
        z = _dot(u[:, lo:lo + FFT_GROUP_DIM].astype(BF16), dft_ref[...])
        zr_ref[:, g * FFT_GROUP_DIM:(g + 1) * FFT_GROUP_DIM] = z[:, :FFT_GROUP_DIM]
        zi_ref[:, g * FFT_GROUP_DIM:(g + 1) * FFT_GROUP_DIM] = z[:, FFT_GROUP_DIM:]


def _group_dft_matrix():
    j = np.arange(FFT_GROUP_DIM)
    ang = 2.0 * np.pi * ((j[:, None] * j[None, :]) % FFT_GROUP_DIM) / FFT_GROUP_DIM
    return jnp.asarray(np.concatenate([np.cos(ang), -np.sin(ang)], axis=1), dtype=F32).astype(BF16)


def _inproj(x2d, mod3, w_in_bf, seq, b0):
    t = x2d.shape[0]
    tm = 512
    row = pl.BlockSpec((tm, D_REC), lambda i: (i, 0))
    return pl.pallas_call(
        _inproj_kernel,
        out_shape=[jax.ShapeDtypeStruct((t, D_REC), F32)] * 4,
        grid=(t // tm,),
        in_specs=[pl.BlockSpec((tm, D_MODEL), lambda i: (i, 0)),
                  pl.BlockSpec((1, 6, D_MODEL), lambda i: (b0 + (i * tm) // seq, 0, 0)),
                  pl.BlockSpec((D_MODEL, 3 * D_REC), lambda i: (0, 0)),
                  pl.BlockSpec((FFT_GROUP_DIM, 2 * FFT_GROUP_DIM), lambda i: (0, 0))],
        out_specs=[row, row, row, row],
        compiler_params=_params(("arbitrary",)),
        name="inproj",
    )(x2d, mod3, w_in_bf, _group_dft_matrix())


REC_CHUNK = 512


def _scan_chunk(a, b, carry, h_ref, reverse):
    rows, width = a.shape
    n_tiles = rows // SUBLANES
    a = a.reshape(n_tiles, SUBLANES, width)
    b = b.reshape(n_tiles, SUBLANES, width)
    ridx = lax.broadcasted_iota(jnp.int32, a.shape, 1)
    for s in (1, 2, 4):
        shift = SUBLANES - s if reverse else s
        keep = ridx < SUBLANES - s if reverse else ridx >= s
        a_sh = pltpu.roll(a, shift, 1)
        b_sh = pltpu.roll(b, shift, 1)
        b = jnp.where(keep, a * b_sh + b, b)
        a = jnp.where(keep, a * a_sh, a)
    order = range(n_tiles - 1, -1, -1) if reverse else range(n_tiles)
    for j in order:
        h = b[j] + a[j] * carry
        carry = h[0:1] if reverse else h[SUBLANES - 1:SUBLANES]
        h_ref[j * SUBLANES:(j + 1) * SUBLANES, :] = h
    return carry


def _conv_chunk(prev_ref, x_ref, next_ref, cw_ref, cb_ref, first, last):
    rows = x_ref.shape[0]
    prev = jnp.where(first, 0.0, prev_ref[...])
    nxt = jnp.where(last, 0.0, next_ref[...])
    ext = jnp.concatenate([prev, x_ref[...], nxt], axis=0)
    n = rows + 2 * SUBLANES
    body = slice(SUBLANES, SUBLANES + rows)
    cw = cw_ref[...]
    xc = cw[2:3] * ext[body]
    xc = xc + cw[0:1] * pltpu.roll(ext, 2, 0)[body]
    xc = xc + cw[1:2] * pltpu.roll(ext, 1, 0)[body]
    xc = xc + cw[3:4] * pltpu.roll(ext, n - 1, 0)[body]
    return xc + cb_ref[...]


def _sigmoid(x):
    return 0.5 * jnp.tanh(0.5 * x) + 0.5


def _neg_expm1_of_double(half_y):
    t = jnp.tanh(half_y)
    return (-2.0 * t) / (1.0 - t)


def _lru_coeffs(xc, wg_ref, bg_ref, lam_ref):
    lam = lam_ref[...]
    neg = -lam
    softplus = jnp.maximum(neg, 0.0) + jnp.log(1.0 + jnp.exp(-jnp.abs(neg)))
    xb = xc.astype(BF16)
    r_parts, i_parts = [], []
    for cb in range(D_REC // LANES):
        g = _dot(xb[:, cb * LANES:(cb + 1) * LANES], wg_ref[cb])
        r_parts.append(g[:, :LANES])
        i_parts.append(g[:, LANES:])
    bg = bg_ref[...]
    r = _sigmoid(jnp.concatenate(r_parts, axis=1) + bg[0:1])
    i = _sigmoid(jnp.concatenate(i_parts, axis=1) + bg[1:2])
    log_a = (-LRU_C) * r * softplus
    a = jnp.exp(log_a)
    mult = jnp.sqrt(_neg_expm1_of_double(log_a))
    return a, mult * (i * xc)


def _rec_fwd_kernel(prev_ref, x_ref, next_ref, cw_ref, cb_ref, wg_ref, bg_ref, lam_ref,
                    hf_ref, carry_ref):
    c = pl.program_id(1)
    nch = pl.num_programs(1)

    @pl.when(c == 0)
    def _():
        carry_ref[...] = jnp.zeros_like(carry_ref)

    xc = _conv_chunk(prev_ref, x_ref, next_ref, cw_ref, cb_ref, c == 0, c == nch - 1)
    a, b = _lru_coeffs(xc, wg_ref, bg_ref, lam_ref)
    carry_ref[...] = _scan_chunk(a, b, carry_ref[...], hf_ref, reverse=False)


def _rec_bwd_kernel(prev_ref, x_ref, next_ref, cw_ref, cb_ref, wg_ref, bg_ref, lam_ref,
                    hf_ref, ug_ref, rec_ref, carry_ref, hb_ref):
    j = pl.program_id(1)
    nch = pl.num_programs(1)

    @pl.when(j == 0)
    def _():
        carry_ref[...] = jnp.zeros_like(carry_ref)

    xc = _conv_chunk(prev_ref, x_ref, next_ref, cw_ref, cb_ref, j == nch - 1, j == 0)
    a, b = _lru_coeffs(xc, wg_ref, bg_ref, lam_ref)
    carry_ref[...] = _scan_chunk(a, b, carry_ref[...], hb_ref, reverse=True)
    ug = ug_ref[...]
    gelu = 0.5 * ug * (1.0 + jnp.tanh(math.sqrt(2.0 / math.pi) * (ug + 0.044715 * (ug * ug * ug))))
    rec_ref[...] = (hf_ref[...] + hb_ref[...]) * gelu


def _rec_specs(batch, seq, reverse):
    lc = min(REC_CHUNK, seq)
    nch = seq // lc
    tiles = lc // SUBLANES
    last_tile = batch * nch * tiles - 1

    def chunk(b, j):
        return nch - 1 - j if reverse else j

    row = pl.BlockSpec((lc, D_REC), lambda b, j: (b * nch + chunk(b, j), 0))
    prev = pl.BlockSpec((SUBLANES, D_REC),
                        lambda b, j: (jnp.maximum((b * nch + chunk(b, j)) * tiles - 1, 0), 0))
    nxt = pl.BlockSpec((SUBLANES, D_REC),
                       lambda b, j: (jnp.minimum((b * nch + chunk(b, j) + 1) * tiles, last_tile), 0))
    const2 = lambda shape: pl.BlockSpec(shape, lambda b, j: (0,) * len(shape))
    weights = [const2((CONV_WIDTH, D_REC)), const2((1, D_REC)),
               const2((D_REC // LANES, LANES, 2 * LANES)), const2((2, D_REC)), const2((1, D_REC))]
    return row, prev, nxt, weights, nch, lc


def _rg_lru(ux, ug, conv_w, conv_b, wg, bg, lam, batch, seq):
    t = ux.shape[0]
    row, prev, nxt, wspecs, nch, lc = _rec_specs(batch, seq, False)
    hf = pl.pallas_call(
        _rec_fwd_kernel,
        out_shape=jax.ShapeDtypeStruct((t, D_REC), F32),
        grid=(batch, nch),
        in_specs=[prev, row, nxt] + wspecs,
        out_specs=row,
        scratch_shapes=[pltpu.VMEM((1, D_REC), F32)],
        compiler_params=_params(("arbitrary", "arbitrary")),
        name="rec_fwd",
    )(ux, ux, ux, conv_w, conv_b, wg[0], bg[0], lam[0:1])
    row, prev, nxt, wspecs, nch, lc = _rec_specs(batch, seq, True)
    return pl.pallas_call(
        _rec_bwd_kernel,
        out_shape=jax.ShapeDtypeStruct((t, D_REC), F32),
        grid=(batch, nch),
        in_specs=[prev, row, nxt] + wspecs + [row, row],
        out_specs=row,
        scratch_shapes=[pltpu.VMEM((1, D_REC), F32), pltpu.VMEM((lc, D_REC), F32)],
        compiler_params=_params(("arbitrary", "arbitrary")),
        name="rec_bwd",
    )(ux, ux, ux, conv_w, conv_b, wg[1], bg[1], lam[1:2], hf, ug)


DFT_A_JB = 8
DFT_C_GB = 2


def _dft_a_kernel(zr_ref, zi_ref, t_ref, a_ref):
    for jb in range(zr_ref.shape[2]):
        for r in range(SUBLANES):
            rhs = jnp.concatenate([zr_ref[0, :, jb, r, :], zi_ref[0, :, jb, r, :]], axis=0)
            a_ref[0, 0, :, jb, r, :] = _dot(t_ref[jb * SUBLANES + r], rhs.astype(BF16))


def _dft_c_kernel(a_ref, m_ref, o_ref, *, scale):
    for g in range(a_ref.shape[3]):
        for j in range(SUBLANES):
            rhs = jnp.concatenate([a_ref[0, 0, 0, g, j], a_ref[0, 0, 1, g, j]], axis=0)
            o_ref[0, :, g, j, :] = _dot(m_ref[...], rhs.astype(BF16)) * scale


def _dft_tables(seq):
    n2 = DFT_N2
    n1 = seq // n2
    k1 = jnp.arange(n1, dtype=jnp.int32)
    s = n2 * jnp.arange(n1, dtype=jnp.int32)[None, None, :] + jnp.arange(n2, dtype=jnp.int32)[:, None, None]
    ang = (2.0 * math.pi / seq) * ((k1[None, :, None] * s) % seq).astype(F32)
    cs, sn = jnp.cos(ang), jnp.sin(ang)
    t_a = jnp.concatenate([jnp.concatenate([cs, sn], axis=2),
                           jnp.concatenate([-sn, cs], axis=2)], axis=1).astype(BF16)
    k2 = jnp.arange(n2, dtype=jnp.int32)
    phi = (2.0 * math.pi / n2) * ((k2[:, None] * k2[None, :]) % n2).astype(F32)
    m_c = jnp.concatenate([jnp.cos(phi), jnp.sin(phi)], axis=1).astype(BF16)
    return t_a, m_c


def _seq_dft(zr, zi, batch, seq):
    n2 = DFT_N2
    n1 = seq // n2
    nj = n2 // SUBLANES
    ncb = D_FFT // LANES
    t_a, m_c = _dft_tables(seq)
    z5 = (batch, n1, nj, SUBLANES, D_FFT)
    jb = min(DFT_A_JB, nj)
    zspec = pl.BlockSpec((1, n1, jb, SUBLANES, LANES), lambda b, cb, j: (b, 0, j, 0, cb))
    a6 = pl.pallas_call(
        _dft_a_kernel,
        out_shape=jax.ShapeDtypeStruct((batch, ncb, 2 * n1, nj, SUBLANES, LANES), F32),
        grid=(batch, ncb, nj // jb),
        in_specs=[zspec, zspec,
                  pl.BlockSpec((jb * SUBLANES, 2 * n1, 2 * n1), lambda b, cb, j: (j, 0, 0))],
        out_specs=pl.BlockSpec((1, 1, 2 * n1, jb, SUBLANES, LANES), lambda b, cb, j: (b, cb, 0, j, 0, 0)),
        compiler_params=_params(("arbitrary",) * 3),
        name="dft_a",
    )(zr.reshape(z5), zi.reshape(z5), t_a)
    ng = n1 // SUBLANES
    gb = min(DFT_C_GB, ng)
    a6 = a6.reshape(batch, ncb, 2, ng, SUBLANES, n2, LANES)
    out = pl.pallas_call(
        functools.partial(_dft_c_kernel, scale=1.0 / math.sqrt(seq * FFT_GROUP_DIM)),
        out_shape=jax.ShapeDtypeStruct((batch, n2, ng, SUBLANES, D_FFT), F32),
        grid=(batch, ncb, ng // gb),
        in_specs=[pl.BlockSpec((1, 1, 2, gb, SUBLANES, n2, LANES), lambda b, cb, g: (b, cb, 0, g, 0, 0, 0)),
                  pl.BlockSpec((n2, 2 * n2), lambda b, cb, g: (0, 0))],
        out_specs=pl.BlockSpec((1, n2, gb, SUBLANES, LANES), lambda b, cb, g: (b, 0, g, 0, cb)),
        compiler_params=_params(("arbitrary",) * 3),
        name="dft_c",
    )(a6, m_c)
    return out.reshape(batch * seq, D_FFT)


ROUTE_TILE = 512


def _mix_route_kernel(rec_ref, four_ref, x_ref, mod_ref, gmix_ref, wout_ref, ln1g_ref, ln1b_ref,
                      wr_ref, br_ref,
                      x1_ref, h2_ref, tope_ref, topw_ref, cnt_ref, run_ref):
    i = pl.program_id(0)

    @pl.when(i == 0)
    def _():
        run_ref[...] = jnp.zeros_like(run_ref)

    m = mod_ref[0]
    gm = gmix_ref[...]
    rec = rec_ref[...]
    four = four_ref[...]
    rn = rec * lax.rsqrt(jnp.mean(rec * rec, axis=-1, keepdims=True) + LN_EPS) * gm[:, :D_REC]
    fn = four * lax.rsqrt(jnp.mean(four * four, axis=-1, keepdims=True) + LN_EPS) * gm[:, D_REC:]
    mix = _dot(rn.astype(BF16), wout_ref[:D_REC, :]) + _dot(fn.astype(BF16), wout_ref[D_REC:, :])
    x1 = _ln(DEEPNORM_ALPHA * x_ref[...] + m[2:3] * mix) * ln1g_ref[...] + ln1b_ref[...]
    x1_ref[...] = x1
    h2 = _ln(x1) * (1.0 + m[4:5]) + m[3:4]
    for j in range(SUBLANES):
        h2_ref[_lane_block(j, h2.shape[0])] = h2[:, j * LANES:(j + 1) * LANES]
    logits = _dot3(h2, wr_ref[...]) + br_ref[...]
    lane = lax.broadcasted_iota(jnp.int32, logits.shape, 1).astype(F32)
    work = logits
    onehot = jnp.zeros(logits.shape, F32)
    vals, idxs = [], []
    for _ in range(TOP_K):
        v = jnp.max(work, axis=-1, keepdims=True)
        e = jnp.min(jnp.where(work == v, lane, float(N_EXPERTS)), axis=-1, keepdims=True)
        sel = lane == e
        onehot = jnp.where(sel, 1.0, onehot)
        work = jnp.where(sel, -jnp.inf, work)
        vals.append(v)
        idxs.append(e)
    ex = [jnp.exp(v - vals[0]) for v in vals]
    denom = ex[0] + ex[1] + ex[2] + ex[3]
    for k in range(TOP_K):
        tope_ref[:, k:k + 1] = idxs[k].astype(jnp.int32)
        topw_ref[:, k:k + 1] = ex[k] / denom
    run_ref[...] = run_ref[...] + jnp.sum(onehot, axis=0, keepdims=True)
    cnt_ref[...] = run_ref[...]


def _mix_route(rec, four, x2d, mod3, g_mix, w_out_bf, ln1_g, ln1_b, w_router, b_router, seq, b0):
    t = x2d.shape[0]
    tm = ROUTE_TILE
    half = pl.BlockSpec((tm, D_REC), lambda i: (i, 0))
    full = pl.BlockSpec((tm, D_MODEL), lambda i: (i, 0))
    vec = pl.BlockSpec((1, D_MODEL), lambda i: (0, 0))
    k4 = pl.BlockSpec((tm, TOP_K), lambda i: (i, 0))
    return pl.pallas_call(
        _mix_route_kernel,
        out_shape=[jax.ShapeDtypeStruct((t, D_MODEL), F32), jax.ShapeDtypeStruct((t * ROW_SUB, LANES), F32),
                   jax.ShapeDtypeStruct((t, TOP_K), jnp.int32), jax.ShapeDtypeStruct((t, TOP_K), F32),
                   jax.ShapeDtypeStruct((1, N_EXPERTS), F32)],
        grid=(t // tm,),
        in_specs=[half, half, full,
                  pl.BlockSpec((1, 6, D_MODEL), lambda i: (b0 + (i * tm) // seq, 0, 0)),
                  vec, pl.BlockSpec((D_MODEL, D_MODEL), lambda i: (0, 0)), vec, vec,
                  pl.BlockSpec((D_MODEL, N_EXPERTS), lambda i: (0, 0)),
                  pl.BlockSpec((1, N_EXPERTS), lambda i: (0, 0))],
        out_specs=[full, pl.BlockSpec((tm * ROW_SUB, LANES), lambda i: (i, 0)), k4, k4,
                   pl.BlockSpec((1, N_EXPERTS), lambda i: (0, 0))],
        scratch_shapes=[pltpu.VMEM((1, N_EXPERTS), F32)],
        compiler_params=_params(("arbitrary",)),
        name="mix_route",
    )(rec, four, x2d, mod3, g_mix.reshape(1, -1), w_out_bf, ln1_g.reshape(1, -1), ln1_b.reshape(1, -1),
      w_router, b_router.reshape(1, -1))


MOE_TILE = 256
IDX_RING = 4


def _moe_kernel(blk_e_ref, nact_ref, asg_hbm, h_hbm, wgu_ref, bgu_ref, wd_ref, bd_ref, out_hbm,
                idx_ref, xbuf_ref, obuf_ref, xb_ref, idx_sem, row_sem, sc_sem, *, n_tok):
    i = pl.program_id(0)
    nact = nact_ref[0]
    tm = MOE_TILE
    last = nact - 1

    def idx_copy(blk, ring):
        return pltpu.make_async_copy(asg_hbm.at[blk], idx_ref.at[ring], idx_sem.at[ring])

    def gather_rows(ring, slot):
        for r in range(tm):
            src = pl.multiple_of((idx_ref[ring, r] & (n_tok - 1)) * ROW_SUB, ROW_SUB)
            pltpu.make_async_copy(h_hbm.at[pl.ds(src, ROW_SUB)], xbuf_ref.at[slot, pl.ds(r * ROW_SUB, ROW_SUB)],
                                  row_sem.at[slot]).start(priority=r % 2)

    def scatter_rows(ring, slot):
        for r in range(tm):
            row = pl.multiple_of(idx_ref[ring, r] * ROW_SUB, ROW_SUB)
            pltpu.make_async_copy(obuf_ref.at[slot, pl.ds(r * ROW_SUB, ROW_SUB)], out_hbm.at[pl.ds(row, ROW_SUB)],
                                  sc_sem.at[slot]).start(priority=r % 2)

    def wait_gather(slot):
        pltpu.make_async_copy(h_hbm.at[pl.ds(0, tm * ROW_SUB)], xbuf_ref.at[slot], row_sem.at[slot]).wait()

    def wait_scatter(slot):
        pltpu.make_async_copy(obuf_ref.at[slot], out_hbm.at[pl.ds(0, tm * ROW_SUB)], sc_sem.at[slot]).wait()

    def step(with_scatter):
        slot = i % 2
        wait_gather(slot)
        idx_copy(0, (i + 1) % IDX_RING).wait()
        idx_copy(jnp.minimum(i + 2, last), (i + 2) % IDX_RING).start()
        if with_scatter:
            @pl.when(i >= 2)
            def _():
                wait_scatter(slot)
        for j in range(SUBLANES):
            xb_ref[:, j * LANES:(j + 1) * LANES] = xbuf_ref[(slot,) + _lane_block(j, tm)].astype(BF16)
        gather_rows((i + 1) % IDX_RING, 1 - slot)
        if with_scatter:
            scatter_rows((i - 1) % IDX_RING, 1 - slot)
        gu = _dot(xb_ref[...], wgu_ref[0]) + bgu_ref[0]
        gate = jnp.minimum(gu[:, :D_FF], SWIGLU_LIMIT)
        up = jnp.clip(gu[:, D_FF:], -SWIGLU_LIMIT, SWIGLU_LIMIT)
        act = (up + 1.0) * (gate * jax.nn.sigmoid(SWIGLU_ALPHA * gate))
        out = _dot(act.astype(BF16), wd_ref[0]) + bd_ref[0]
        for j in range(SUBLANES):
            obuf_ref[(slot,) + _lane_block(j, tm)] = out[:, j * LANES:(j + 1) * LANES]

    @pl.when(i == 0)
    def _():
        obuf_ref[0] = jnp.zeros(obuf_ref.shape[1:], F32)
        spare = pltpu.make_async_copy(obuf_ref.at[0], out_hbm.at[pl.ds(TOP_K * n_tok * ROW_SUB, tm * ROW_SUB)],
                                      sc_sem.at[0])
        spare.start()
        spare.wait()
        cp = idx_copy(0, 0)
        cp.start()
        cp.wait()
        gather_rows(0, 0)
        idx_copy(jnp.minimum(1, last), 1).start()
        step(False)

    @pl.when(jnp.logical_and(i > 0, i < nact))
    def _():
        step(True)

    @pl.when(i == nact)
    def _():
        wait_gather(i % 2)
        idx_copy(0, (i + 1) % IDX_RING).wait()

        @pl.when(i >= 2)
        def _():
            wait_scatter(i % 2)

        scatter_rows((i - 1) % IDX_RING, (i - 1) % 2)
        wait_scatter((i - 1) % 2)


def _moe(h2, slot_asg, blk_e, nact, wgu_bf, b_gate_up, wd_bf, b_down, out_rows):
    n_blocks = blk_e.shape[0]
    n_tok = h2.shape[0] // ROW_SUB
    assert n_tok & (n_tok - 1) == 0, "token count must be a power of two"
    tm = MOE_TILE
    grid_spec = pltpu.PrefetchScalarGridSpec(
        num_scalar_prefetch=2,
        grid=(n_blocks,),
        in_specs=[pl.BlockSpec(memory_space=pl.ANY),
                  pl.BlockSpec(memory_space=pl.ANY),
                  pl.BlockSpec((1, D_MODEL, 2 * D_FF), lambda i, be, na: (be[i], 0, 0)),
                  pl.BlockSpec((1, 1, 2 * D_FF), lambda i, be, na: (be[i], 0, 0)),
                  pl.BlockSpec((1, D_FF, D_MODEL), lambda i, be, na: (be[i], 0, 0)),
                  pl.BlockSpec((1, 1, D_MODEL), lambda i, be, na: (be[i], 0, 0))],
        out_specs=pl.BlockSpec(memory_space=pl.ANY),
        scratch_shapes=[pltpu.SMEM((IDX_RING, tm), jnp.int32),
                        pltpu.VMEM((2, tm * ROW_SUB, LANES), F32),
                        pltpu.VMEM((2, tm * ROW_SUB, LANES), F32),
                        pltpu.VMEM((tm, D_MODEL), BF16),
                        pltpu.SemaphoreType.DMA((IDX_RING,)),
                        pltpu.SemaphoreType.DMA((2,)),
                        pltpu.SemaphoreType.DMA((2,))],
    )
    return pl.pallas_call(
        functools.partial(_moe_kernel, n_tok=n_tok),
        out_shape=jax.ShapeDtypeStruct((out_rows * ROW_SUB, LANES), F32),
        grid_spec=grid_spec,
        compiler_params=pltpu.CompilerParams(dimension_semantics=("arbitrary",), vmem_limit_bytes=VMEM_LIMIT,
                                             disable_bounds_checks=True),
        name="moe",
    )(blk_e, nact, slot_asg.reshape(n_blocks, tm), h2, wgu_bf, b_gate_up.reshape(N_EXPERTS, 1, -1),
      wd_bf, b_down.reshape(N_EXPERTS, 1, -1))


COMBINE_TILE = 512


def _combine_kernel(o0_ref, o1_ref, o2_ref, o3_ref, x1_ref, w_ref, mod_ref, g_ref, b_ref, o_ref):
    w = w_ref[...]
    parts = []
    for j in range(SUBLANES):
        blk = _lane_block(j, w.shape[0])
        parts.append(w[:, 0:1] * o0_ref[blk] + w[:, 1:2] * o1_ref[blk]
                     + w[:, 2:3] * o2_ref[blk] + w[:, 3:4] * o3_ref[blk])
    y = jnp.concatenate(parts, axis=1)
    m = mod_ref[0]
    o_ref[...] = _ln(DEEPNORM_ALPHA * x1_ref[...] + m[5:6] * y) * g_ref[...] + b_ref[...]


def _combine(outs, x1, top_w, mod3, ln2_g, ln2_b, seq, b0):
    t = x1.shape[0]
    tk = COMBINE_TILE
    nt = t // tk
    full = pl.BlockSpec((tk, D_MODEL), lambda i: (i, 0))
    vec = pl.BlockSpec((1, D_MODEL), lambda i: (0, 0))
    expert_rows = [pl.BlockSpec((tk * ROW_SUB, LANES), lambda i, k=k: (k * nt + i, 0)) for k in range(TOP_K)]
    return pl.pallas_call(
        _combine_kernel,
        out_shape=jax.ShapeDtypeStruct((t, D_MODEL), F32),
        grid=(nt,),
        in_specs=expert_rows + [full, pl.BlockSpec((tk, TOP_K), lambda i: (i, 0)),
                                pl.BlockSpec((1, 6, D_MODEL), lambda i: (b0 + (i * tk) // seq, 0, 0)),
                                vec, vec],
        out_specs=full,
        compiler_params=_params(("arbitrary",)),
        name="combine",
    )(outs, outs, outs, outs, x1, top_w, mod3, ln2_g.reshape(1, -1), ln2_b.reshape(1, -1))


def _gate_weights(w_rg, b_rg):
    hpb = LANES // REC_HEAD_DIM
    nblk = D_REC // LANES
    w = w_rg.reshape(2, 2, nblk, hpb, REC_HEAD_DIM, REC_HEAD_DIM)
    eye = jnp.eye(hpb, dtype=w.dtype)
    blk = jnp.einsum('dgbhij,hk->dgbhikj', w, eye).reshape(2, 2, nblk, LANES, LANES)
    wg = jnp.concatenate([blk[:, 0], blk[:, 1]], axis=-1).astype(BF16)
    bg = b_rg.reshape(2, 2, D_REC)
    return wg, bg


def _routing_tables(top_e, counts, tm):
    t = top_e.shape[0]
    n_asg = t * TOP_K
    assert t & (t - 1) == 0 and TOP_K == 4
    counts = counts.reshape(N_EXPERTS).astype(jnp.int32)
    padded = (counts + tm - 1) // tm * tm
    pad_ends = jnp.cumsum(padded)
    n_blocks = n_asg // tm + N_EXPERTS
    blk_start = jnp.arange(n_blocks, dtype=jnp.int32) * tm
    blk_e = jnp.minimum(jnp.sum((pad_ends[None, :] <= blk_start[:, None]).astype(jnp.int32), axis=1),
                        N_EXPERTS - 1)
    nact = (pad_ends[-1] // tm).astype(jnp.int32).reshape(1)
    tok = jnp.arange(t, dtype=jnp.int32)[None, :]
    kk = jnp.arange(TOP_K, dtype=jnp.int32)[:, None]
    real = (top_e.T * (2 * t) + tok) * TOP_K + kk
    e = jnp.arange(N_EXPERTS, dtype=jnp.int32)[:, None]
    m = jnp.arange(tm, dtype=jnp.int32)[None, :]
    unused = (N_EXPERTS * 2 * t + t) * TOP_K
    pads = jnp.where(m < (padded - counts)[:, None], (e * (2 * t) + t + m) * TOP_K, unused)
    key = jnp.sort(jnp.concatenate([real.reshape(-1), pads.reshape(-1)]))
    field = lax.shift_right_logical(key, 2) & (2 * t - 1)
    slot = jnp.arange(n_blocks * tm, dtype=jnp.int32)
    slot_asg = jnp.where(field >= t, n_asg + (slot & (tm - 1)), (key & (TOP_K - 1)) * t + field)
    return slot_asg, blk_e, nact


def _trunk(x, mod3, b0, p):
    batch, seq, _ = x.shape
    t = batch * seq
    x2d = x.reshape(t, D_MODEL)
    ux, ug, zr, zi = _inproj(x2d, mod3, p['w_in_bf'], seq, b0)
    rec = _rg_lru(ux, ug, p['conv_w'], p['conv_b'], p['wg'], p['bg'], p['lam'], batch, seq)
    four = _seq_dft(zr, zi, batch, seq)
    x1, h2, top_e, top_w, counts = _mix_route(rec, four, x2d, mod3, p['g_mix'], p['w_out_bf'],
                                                    p['ln1_g'], p['ln1_b'], p['w_router'], p['b_router'],
                                                    seq, b0)
    slot_asg, blk_e, nact = _routing_tables(top_e, counts, MOE_TILE)
    outs = _moe(h2, slot_asg, blk_e, nact, p['wgu_bf'], p['b_gate_up'], p['wd_bf'], p['b_down'],
                out_rows=t * TOP_K + MOE_TILE)
    y = _combine(outs, x1, top_w, mod3, p['ln2_g'], p['ln2_b'], seq, b0)
    return y.reshape(batch, seq, D_MODEL)


def kernel(x_prompt, x_sample, c_prompt, c_sample, w_ada, b_ada, w_in, conv_w, conv_b, w_rg, b_rg,
           lru_lambda, g_mix, w_out, ln1_g, ln1_b, w_router, b_router, w_gate_up, b_gate_up,
           w_down, b_down, ln2_g, ln2_b):
    depth = w_ada.shape[0]
    xs = [x_prompt, x_sample]
    cs = [c_prompt, c_sample]
    nb = [c.shape[0] for c in cs]
    bp = -(-sum(nb) // SUBLANES) * SUBLANES
    for l in range(depth):
        wg, bg = _gate_weights(w_rg[l], b_rg[l])
        p = dict(w_in_bf=w_in[l].astype(BF16), conv_w=conv_w[l], conv_b=conv_b[l].reshape(1, -1),
                 wg=wg, bg=bg, lam=lru_lambda[l], g_mix=g_mix[l], w_out_bf=w_out[l].astype(BF16),
                 ln1_g=ln1_g[l], ln1_b=ln1_b[l], w_router=w_router[l], b_router=b_router[l],
                 wgu_bf=w_gate_up[l].astype(BF16), b_gate_up=b_gate_up[l],
                 wd_bf=w_down[l].astype(BF16), b_down=b_down[l], ln2_g=ln2_g[l], ln2_b=ln2_b[l])
        c_all = jnp.concatenate(cs + [jnp.zeros((bp - sum(nb), D_MODEL), F32)], axis=0)
        mod3 = _ada_mod(c_all, w_ada[l], b_ada[l]).reshape(bp, 6, D_MODEL)
        xs = [_trunk(xs[0], mod3, 0, p), _trunk(xs[1], mod3, nb[0], p)]
    return (xs[0], xs[1])
```

```python
import functools
import math

import numpy as np
import jax
import jax.numpy as jnp
from jax import lax
from jax.experimental import pallas as pl
from jax.experimental.pallas import tpu as pltpu

D_MODEL = 1024
D_REC = 512
D_FFT = 512
N_REC_HEADS = 8
REC_HEAD_DIM = 64
FFT_GROUP_DIM = 128
N_FFT_GROUPS = 4
CONV_WIDTH = 4
CONV_LEFT = 2
LRU_C = 8.0
N_EXPERTS = 32
TOP_K = 4
D_FF = 1024
SWIGLU_LIMIT = 7.0
SWIGLU_ALPHA = 1.702
DEEPNORM_ALPHA = 2.0 ** 0.25
LN_EPS = 1e-5

LANES = 128
SUBLANES = 8
ROW_SUB = D_MODEL // LANES
assert ROW_SUB == SUBLANES
DFT_N2 = 128
VMEM_LIMIT = 48 * 1024 * 1024

F32 = jnp.float32
BF16 = jnp.bfloat16


def _lane_block(j, n):
    return (pl.ds(j, n, stride=ROW_SUB), slice(None))


def _params(sem, vmem=VMEM_LIMIT):
    return pltpu.CompilerParams(dimension_semantics=sem, vmem_limit_bytes=vmem)


def _dot(a, b):
    return jnp.dot(a, b, preferred_element_type=F32)


def _dot3(a, b):
    a_hi = a.astype(BF16)
    b_hi = b.astype(BF16)
    a_lo = (a - a_hi.astype(F32)).astype(BF16)
    b_lo = (b - b_hi.astype(F32)).astype(BF16)
    return _dot(a_hi, b_hi) + _dot(a_lo, b_hi) + _dot(a_hi, b_lo)


def _ln(x):
    mu = jnp.mean(x, axis=-1, keepdims=True)
    xc = x - mu
    var = jnp.mean(xc * xc, axis=-1, keepdims=True)
    return xc * lax.rsqrt(var + LN_EPS)


def _ada_kernel(c_ref, w_ref, b_ref, o_ref):
    c = c_ref[...]
    s = c * jax.nn.sigmoid(c)
    o_ref[...] = _dot3(s, w_ref[...]) + b_ref[...]


def _ada_mod(c_all, w_ada, b_ada):
    bp = c_all.shape[0]
    n = w_ada.shape[1]
    tn = 768
    return pl.pallas_call(
        _ada_kernel,
        out_shape=jax.ShapeDtypeStruct((bp, n), F32),
        grid=(n // tn,),
        in_specs=[pl.BlockSpec((bp, D_MODEL), lambda j: (0, 0)),
                  pl.BlockSpec((D_MODEL, tn), lambda j: (0, j)),
                  pl.BlockSpec((1, tn), lambda j: (0, j))],
        out_specs=pl.BlockSpec((bp, tn), lambda j: (0, j)),
        compiler_params=_params(("arbitrary",)),
        name="ada_mod",
    )(c_all, w_ada, b_ada.reshape(1, n))


def _inproj_kernel(x_ref, mod_ref, w_ref, dft_ref, ux_ref, ug_ref, zr_ref, zi_ref):
    m = mod_ref[0]
    h = _ln(x_ref[...]) * (1.0 + m[1:2]) + m[0:1]
    u = _dot(h.astype(BF16), w_ref[...])
    ux_ref[...] = u[:, :D_REC]
    ug_ref[...] = u[:, D_REC:2 * D_REC]
    for g in range(N_FFT_GROUPS):
        lo = 2 * D_REC + g * FFT_GROUP_DIM
        z = _dot(u[:, lo:lo + FFT_GROUP_DIM].astype(BF16), dft_ref[...])
        zr_ref[:, g * FFT_GROUP_DIM:(g + 1) * FFT_GROUP_DIM] = z[:, :FFT_GROUP_DIM]
        zi_ref[:, g * FFT_GROUP_DIM:(g + 1) * FFT_GROUP_DIM] = z[:, FFT_GROUP_DIM:]


def _group_dft_matrix():
    j = np.arange(FFT_GROUP_DIM)
    ang = 2.0 * np.pi * ((j[:, None] * j[None, :]) % FFT_GROUP_DIM) / FFT_GROUP_DIM
    return jnp.asarray(np.concatenate([np.cos(ang), -np.sin(ang)], axis=1), dtype=F32).astype(BF16)


def _inproj(x2d, mod3, w_in_bf, seq, b0):
    t = x2d.shape[0]
    tm = 512
    row = pl.BlockSpec((tm, D_REC), lambda i: (i, 0))
    return pl.pallas_call(
        _inproj_kernel,
        out_shape=[jax.ShapeDtypeStruct((t, D_REC), F32)] * 4,
        grid=(t // tm,),
        in_specs=[pl.BlockSpec((tm, D_MODEL), lambda i: (i, 0)),
                  pl.BlockSpec((1, 6, D_MODEL), lambda i: (b0 + (i * tm) // seq, 0, 0)),
                  pl.BlockSpec((D_MODEL, 3 * D_REC), lambda i: (0, 0)),
                  pl.BlockSpec((FFT_GROUP_DIM, 2 * FFT_GROUP_DIM), lambda i: (0, 0))],
        out_specs=[row, row, row, row],
        compiler_params=_params(("arbitrary",)),
        name="inproj",
    )(x2d, mod3, w_in_bf, _group_dft_matrix())


REC_CHUNK = 512


def _scan_chunk(a, b, carry, h_ref, reverse):
    rows, width = a.shape
    n_tiles = rows // SUBLANES
    a = a.reshape(n_tiles, SUBLANES, width)
    b = b.reshape(n_tiles, SUBLANES, width)
    ridx = lax.broadcasted_iota(jnp.int32, a.shape, 1)
    for s in (1, 2, 4):
        shift = SUBLANES - s if reverse else s
        keep = ridx < SUBLANES - s if reverse else ridx >= s
        a_sh = pltpu.roll(a, shift, 1)
        b_sh = pltpu.roll(b, shift, 1)
        b = jnp.where(keep, a * b_sh + b, b)
        a = jnp.where(keep, a * a_sh, a)
    order = range(n_tiles - 1, -1, -1) if reverse else range(n_tiles)
    for j in order:
        h = b[j] + a[j] * carry
        carry = h[0:1] if reverse else h[SUBLANES - 1:SUBLANES]
        h_ref[j * SUBLANES:(j + 1) * SUBLANES, :] = h
    return carry


def _conv_chunk(prev_ref, x_ref, next_ref, cw_ref, cb_ref, first, last):
    rows = x_ref.shape[0]
    prev = jnp.where(first, 0.0, prev_ref[...])
    nxt = jnp.where(last, 0.0, next_ref[...])
    ext = jnp.concatenate([prev, x_ref[...], nxt], axis=0)
    n = rows + 2 * SUBLANES
    body = slice(SUBLANES, SUBLANES + rows)
    cw = cw_ref[...]
    xc = cw[2:3] * ext[body]
    xc = xc + cw[0:1] * pltpu.roll(ext, 2, 0)[body]
    xc = xc + cw[1:2] * pltpu.roll(ext, 1, 0)[body]
    xc = xc + cw[3:4] * pltpu.roll(ext, n - 1, 0)[body]
    return xc + cb_ref[...]


def _sigmoid(x):
    return 0.5 * jnp.tanh(0.5 * x) + 0.5


def _neg_expm1_of_double(half_y):
    t = jnp.tanh(half_y)
    return (-2.0 * t) / (1.0 - t)


def _lru_coeffs(xc, wg_ref, bg_ref, lam_ref):
    lam = lam_ref[...]
    neg = -lam
    softplus = jnp.maximum(neg, 0.0) + jnp.log(1.0 + jnp.exp(-jnp.abs(neg)))
    xb = xc.astype(BF16)
    r_parts, i_parts = [], []
    for cb in range(D_REC // LANES):
        g = _dot(xb[:, cb * LANES:(cb + 1) * LANES], wg_ref[cb])
        r_parts.append(g[:, :LANES])
        i_parts.append(g[:, LANES:])
    bg = bg_ref[...]
    r = _sigmoid(jnp.concatenate(r_parts, axis=1) + bg[0:1])
    i = _sigmoid(jnp.concatenate(i_parts, axis=1) + bg[1:2])
    log_a = (-LRU_C) * r * softplus
    a = jnp.exp(log_a)
    mult = jnp.sqrt(_neg_expm1_of_double(log_a))
    return a, mult * (i * xc)


def _rec_fwd_kernel(prev_ref, x_ref, next_ref, cw_ref, cb_ref, wg_ref, bg_ref, lam_ref,
                    hf_ref, carry_ref):
    c = pl.program_id(1)
    nch = pl.num_programs(1)

    @pl.when(c == 0)
    def _():
        carry_ref[...] = jnp.zeros_like(carry_ref)

    xc = _conv_chunk(prev_ref, x_ref, next_ref, cw_ref, cb_ref, c == 0, c == nch - 1)
    a, b = _lru_coeffs(xc, wg_ref, bg_ref, lam_ref)
    carry_ref[...] = _scan_chunk(a, b, carry_ref[...], hf_ref, reverse=False)


def _rec_bwd_kernel(prev_ref, x_ref, next_ref, cw_ref, cb_ref, wg_ref, bg_ref, lam_ref,
                    hf_ref, ug_ref, rec_ref, carry_ref, hb_ref):
    j = pl.program_id(1)
    nch = pl.num_programs(1)

    @pl.when(j == 0)
    def _():
        carry_ref[...] = jnp.zeros_like(carry_ref)

    xc = _conv_chunk(prev_ref, x_ref, next_ref, cw_ref, cb_ref, j == nch - 1, j == 0)
    a, b = _lru_coeffs(xc, wg_ref, bg_ref, lam_ref)
    carry_ref[...] = _scan_chunk(a, b, carry_ref[...], hb_ref, reverse=True)
    ug = ug_ref[...]
    gelu = 0.5 * ug * (1.0 + jnp.tanh(math.sqrt(2.0 / math.pi) * (ug + 0.044715 * (ug * ug * ug))))
    rec_ref[...] = (hf_ref[...] + hb_ref[...]) * gelu


def _rec_specs(batch, seq, reverse):
    lc = min(REC_CHUNK, seq)
    nch = seq // lc
    tiles = lc // SUBLANES
    last_tile = batch * nch * tiles - 1

    def chunk(b, j):
        return nch - 1 - j if reverse else j

    row = pl.BlockSpec((lc, D_REC), lambda b, j: (b * nch + chunk(b, j), 0))
    prev = pl.BlockSpec((SUBLANES, D_REC),
                        lambda b, j: (jnp.maximum((b * nch + chunk(b, j)) * tiles - 1, 0), 0))
    nxt = pl.BlockSpec((SUBLANES, D_REC),
                       lambda b, j: (jnp.minimum((b * nch + chunk(b, j) + 1) * tiles, last_tile), 0))
    const2 = lambda shape: pl.BlockSpec(shape, lambda b, j: (0,) * len(shape))
    weights = [const2((CONV_WIDTH, D_REC)), const2((1, D_REC)),
               const2((D_REC // LANES, LANES, 2 * LANES)), const2((2, D_REC)), const2((1, D_REC))]
    return row, prev, nxt, weights, nch, lc


def _rg_lru(ux, ug, conv_w, conv_b, wg, bg, lam, batch, seq):
    t = ux.shape[0]
    row, prev, nxt, wspecs, nch, lc = _rec_specs(batch, seq, False)
    hf = pl.pallas_call(
        _rec_fwd_kernel,
        out_shape=jax.ShapeDtypeStruct((t, D_REC), F32),
        grid=(batch, nch),
        in_specs=[prev, row, nxt] + wspecs,
        out_specs=row,
        scratch_shapes=[pltpu.VMEM((1, D_REC), F32)],
        compiler_params=_params(("arbitrary", "arbitrary")),
        name="rec_fwd",
    )(ux, ux, ux, conv_w, conv_b, wg[0], bg[0], lam[0:1])
    row, prev, nxt, wspecs, nch, lc = _rec_specs(batch, seq, True)
    return pl.pallas_call(
        _rec_bwd_kernel,
        out_shape=jax.ShapeDtypeStruct((t, D_REC), F32),
        grid=(batch, nch),
        in_specs=[prev, row, nxt] + wspecs + [row, row],
        out_specs=row,
        scratch_shapes=[pltpu.VMEM((1, D_REC), F32), pltpu.VMEM((lc, D_REC), F32)],
        compiler_params=_params(("arbitrary", "arbitrary")),
        name="rec_bwd",
    )(ux, ux, ux, conv_w, conv_b, wg[1], bg[1], lam[1:2], hf, ug)


DFT_A_JB = 8
DFT_C_GB = 2


def _dft_a_kernel(zr_ref, zi_ref, t_ref, a_ref):
    for jb in range(zr_ref.shape[2]):
        for r in range(SUBLANES):
            rhs = jnp.concatenate([zr_ref[0, :, jb, r, :], zi_ref[0, :, jb, r, :]], axis=0)
            a_ref[0, 0, :, jb, r, :] = _dot(t_ref[jb * SUBLANES + r], rhs.astype(BF16))


def _dft_c_kernel(a_ref, m_ref, o_ref, *, scale):
    for g in range(a_ref.shape[3]):
        for j in range(SUBLANES):
            rhs = jnp.concatenate([a_ref[0, 0, 0, g, j], a_ref[0, 0, 1, g, j]], axis=0)
            o_ref[0, :, g, j, :] = _dot(m_ref[...], rhs.astype(BF16)) * scale


def _dft_tables(seq):
    n2 = DFT_N2
    n1 = seq // n2
    k1 = jnp.arange(n1, dtype=jnp.int32)
    s = n2 * jnp.arange(n1, dtype=jnp.int32)[None, None, :] + jnp.arange(n2, dtype=jnp.int32)[:, None, None]
    ang = (2.0 * math.pi / seq) * ((k1[None, :, None] * s) % seq).astype(F32)
    cs, sn = jnp.cos(ang), jnp.sin(ang)
    t_a = jnp.concatenate([jnp.concatenate([cs, sn], axis=2),
                           jnp.concatenate([-sn, cs], axis=2)], axis=1).astype(BF16)
    k2 = jnp.arange(n2, dtype=jnp.int32)
    phi = (2.0 * math.pi / n2) * ((k2[:, None] * k2[None, :]) % n2).astype(F32)
    m_c = jnp.concatenate([jnp.cos(phi), jnp.sin(phi)], axis=1).astype(BF16)
    return t_a, m_c


def _seq_dft(zr, zi, batch, seq):
    n2 = DFT_N2
    n1 = seq // n2
    nj = n2 // SUBLANES
    ncb = D_FFT // LANES
    t_a, m_c = _dft_tables(seq)
    z5 = (batch, n1, nj, SUBLANES, D_FFT)
    jb = min(DFT_A_JB, nj)
    zspec = pl.BlockSpec((1, n1, jb, SUBLANES, LANES), lambda b, cb, j: (b, 0, j, 0, cb))
    a6 = pl.pallas_call(
        _dft_a_kernel,
        out_shape=jax.ShapeDtypeStruct((batch, ncb, 2 * n1, nj, SUBLANES, LANES), F32),
        grid=(batch, ncb, nj // jb),
        in_specs=[zspec, zspec,
                  pl.BlockSpec((jb * SUBLANES, 2 * n1, 2 * n1), lambda b, cb, j: (j, 0, 0))],
        out_specs=pl.BlockSpec((1, 1, 2 * n1, jb, SUBLANES, LANES), lambda b, cb, j: (b, cb, 0, j, 0, 0)),
        compiler_params=_params(("arbitrary",) * 3),
        name="dft_a",
    )(zr.reshape(z5), zi.reshape(z5), t_a)
    ng = n1 // SUBLANES
    gb = min(DFT_C_GB, ng)
    a6 = a6.reshape(batch, ncb, 2, ng, SUBLANES, n2, LANES)
    out = pl.pallas_call(
        functools.partial(_dft_c_kernel, scale=1.0 / math.sqrt(seq * FFT_GROUP_DIM)),
        out_shape=jax.ShapeDtypeStruct((batch, n2, ng, SUBLANES, D_FFT), F32),
        grid=(batch, ncb, ng // gb),
        in_specs=[pl.BlockSpec((1, 1, 2, gb, SUBLANES, n2, LANES), lambda b, cb, g: (b, cb, 0, g, 0, 0, 0)),
                  pl.BlockSpec((n2, 2 * n2), lambda b, cb, g: (0, 0))],
        out_specs=pl.BlockSpec((1, n2, gb, SUBLANES, LANES), lambda b, cb, g: (b, 0, g, 0, cb)),
        compiler_params=_params(("arbitrary",) * 3),
        name="dft_c",
    )(a6, m_c)
    return out.reshape(batch * seq, D_FFT)


ROUTE_TILE = 512


def _mix_route_kernel(rec_ref, four_ref, x_ref, mod_ref, gmix_ref, wout_ref, ln1g_ref, ln1b_ref,
                      wr_ref, br_ref,
                      x1_ref, h2_ref, tope_ref, topw_ref, cnt_ref, run_ref):
    i = pl.program_id(0)

    @pl.when(i == 0)
    def _():
        run_ref[...] = jnp.zeros_like(run_ref)

    m = mod_ref[0]
    gm = gmix_ref[...]
    rec = rec_ref[...]
    four = four_ref[...]
    rn = rec * lax.rsqrt(jnp.mean(rec * rec, axis=-1, keepdims=True) + LN_EPS) * gm[:, :D_REC]
    fn = four * lax.rsqrt(jnp.mean(four * four, axis=-1, keepdims=True) + LN_EPS) * gm[:, D_REC:]
    mix = _dot(rn.astype(BF16), wout_ref[:D_REC, :]) + _dot(fn.astype(BF16), wout_ref[D_REC:, :])
    x1 = _ln(DEEPNORM_ALPHA * x_ref[...] + m[2:3] * mix) * ln1g_ref[...] + ln1b_ref[...]
    x1_ref[...] = x1
    h2 = _ln(x1) * (1.0 + m[4:5]) + m[3:4]
    for j in range(SUBLANES):
        h2_ref[_lane_block(j, h2.shape[0])] = h2[:, j * LANES:(j + 1) * LANES]
    logits = _dot3(h2, wr_ref[...]) + br_ref[...]
    lane = lax.broadcasted_iota(jnp.int32, logits.shape, 1).astype(F32)
    work = logits
    onehot = jnp.zeros(logits.shape, F32)
    vals, idxs = [], []
    for _ in range(TOP_K):
        v = jnp.max(work, axis=-1, keepdims=True)
        e = jnp.min(jnp.where(work == v, lane, float(N_EXPERTS)), axis=-1, keepdims=True)
        sel = lane == e
        onehot = jnp.where(sel, 1.0, onehot)
        work = jnp.where(sel, -jnp.inf, work)
        vals.append(v)
        idxs.append(e)
    ex = [jnp.exp(v - vals[0]) for v in vals]
    denom = ex[0] + ex[1] + ex[2] + ex[3]
    for k in range(TOP_K):
        tope_ref[:, k:k + 1] = idxs[k].astype(jnp.int32)
        topw_ref[:, k:k + 1] = ex[k] / denom
    run_ref[...] = run_ref[...] + jnp.sum(onehot, axis=0, keepdims=True)
    cnt_ref[...] = run_ref[...]


def _mix_route(rec, four, x2d, mod3, g_mix, w_out_bf, ln1_g, ln1_b, w_router, b_router, seq, b0):
    t = x2d.shape[0]
    tm = ROUTE_TILE
    half = pl.BlockSpec((tm, D_REC), lambda i: (i, 0))
    full = pl.BlockSpec((tm, D_MODEL), lambda i: (i, 0))
    vec = pl.BlockSpec((1, D_MODEL), lambda i: (0, 0))
    k4 = pl.BlockSpec((tm, TOP_K), lambda i: (i, 0))
    return pl.pallas_call(
        _mix_route_kernel,
        out_shape=[jax.ShapeDtypeStruct((t, D_MODEL), F32), jax.ShapeDtypeStruct((t * ROW_SUB, LANES), F32),
                   jax.ShapeDtypeStruct((t, TOP_K), jnp.int32), jax.ShapeDtypeStruct((t, TOP_K), F32),
                   jax.ShapeDtypeStruct((1, N_EXPERTS), F32)],
        grid=(t // tm,),
        in_specs=[half, half, full,
                  pl.BlockSpec((1, 6, D_MODEL), lambda i: (b0 + (i * tm) // seq, 0, 0)),
                  vec, pl.BlockSpec((D_MODEL, D_MODEL), lambda i: (0, 0)), vec, vec,
                  pl.BlockSpec((D_MODEL, N_EXPERTS), lambda i: (0, 0)),
                  pl.BlockSpec((1, N_EXPERTS), lambda i: (0, 0))],
        out_specs=[full, pl.BlockSpec((tm * ROW_SUB, LANES), lambda i: (i, 0)), k4, k4,
                   pl.BlockSpec((1, N_EXPERTS), lambda i: (0, 0))],
        scratch_shapes=[pltpu.VMEM((1, N_EXPERTS), F32)],
        compiler_params=_params(("arbitrary",)),
        name="mix_route",
    )(rec, four, x2d, mod3, g_mix.reshape(1, -1), w_out_bf, ln1_g.reshape(1, -1), ln1_b.reshape(1, -1),
      w_router, b_router.reshape(1, -1))


MOE_TILE = 512
IDX_RING = 4


def _moe_kernel(blk_e_ref, nact_ref, asg_hbm, h_hbm, wgu_ref, bgu_ref, wd_ref, bd_ref, out_hbm,
                idx_ref, xbuf_ref, obuf_ref, xb_ref, idx_sem, row_sem, sc_sem, *, n_tok):
    i = pl.program_id(0)
    nact = nact_ref[0]
    tm = MOE_TILE
    last = nact - 1

    def idx_copy(blk, ring):
        return pltpu.make_async_copy(asg_hbm.at[blk], idx_ref.at[ring], idx_sem.at[ring])

    def gather_rows(ring, slot):
        for r in range(tm):
            src = pl.multiple_of((idx_ref[ring, r] & (n_tok - 1)) * ROW_SUB, ROW_SUB)
            pltpu.make_async_copy(h_hbm.at[pl.ds(src, ROW_SUB)], xbuf_ref.at[slot, pl.ds(r * ROW_SUB, ROW_SUB)],
                                  row_sem.at[slot]).start(priority=r % 2)

    def scatter_rows(ring, slot):
        for r in range(tm):
            row = pl.multiple_of(idx_ref[ring, r] * ROW_SUB, ROW_SUB)
            pltpu.make_async_copy(obuf_ref.at[slot, pl.ds(r * ROW_SUB, ROW_SUB)], out_hbm.at[pl.ds(row, ROW_SUB)],
                                  sc_sem.at[slot]).start(priority=r % 2)

    def wait_gather(slot):
        pltpu.make_async_copy(h_hbm.at[pl.ds(0, tm * ROW_SUB)], xbuf_ref.at[slot], row_sem.at[slot]).wait()

    def wait_scatter(slot):
        pltpu.make_async_copy(obuf_ref.at[slot], out_hbm.at[pl.ds(0, tm * ROW_SUB)], sc_sem.at[slot]).wait()

    def step(with_scatter):
        slot = i % 2
        wait_gather(slot)
        idx_copy(0, (i + 1) % IDX_RING).wait()
        idx_copy(jnp.minimum(i + 2, last), (i + 2) % IDX_RING).start()
        if with_scatter:
            @pl.when(i >= 2)
            def _():
                wait_scatter(slot)
        for j in range(SUBLANES):
            xb_ref[:, j * LANES:(j + 1) * LANES] = xbuf_ref[(slot,) + _lane_block(j, tm)].astype(BF16)
        gather_rows((i + 1) % IDX_RING, 1 - slot)
        if with_scatter:
            scatter_rows((i - 1) % IDX_RING, 1 - slot)
        gu = _dot(xb_ref[...], wgu_ref[0]) + bgu_ref[0]
        gate = jnp.minimum(gu[:, :D_FF], SWIGLU_LIMIT)
        up = jnp.clip(gu[:, D_FF:], -SWIGLU_LIMIT, SWIGLU_LIMIT)
        act = (up + 1.0) * (gate * jax.nn.sigmoid(SWIGLU_ALPHA * gate))
        out = _dot(act.astype(BF16), wd_ref[0]) + bd_ref[0]
        for j in range(SUBLANES):
            obuf_ref[(slot,) + _lane_block(j, tm)] = out[:, j * LANES:(j + 1) * LANES]

    @pl.when(i == 0)
    def _():
        obuf_ref[0] = jnp.zeros(obuf_ref.shape[1:], F32)
        spare = pltpu.make_async_copy(obuf_ref.at[0], out_hbm.at[pl.ds(TOP_K * n_tok * ROW_SUB, tm * ROW_SUB)],
                                      sc_sem.at[0])
        spare.start()
        spare.wait()
        cp = idx_copy(0, 0)
        cp.start()
        cp.wait()
        gather_rows(0, 0)
        idx_copy(jnp.minimum(1, last), 1).start()
        step(False)

    @pl.when(jnp.logical_and(i > 0, i < nact))
    def _():
        step(True)

    @pl.when(i == nact)
    def _():
        wait_gather(i % 2)
        idx_copy(0, (i + 1) % IDX_RING).wait()

        @pl.when(i >= 2)
        def _():
            wait_scatter(i % 2)

        scatter_rows((i - 1) % IDX_RING, (i - 1) % 2)
        wait_scatter((i - 1) % 2)


def _moe(h2, slot_asg, blk_e, nact, wgu_bf, b_gate_up, wd_bf, b_down, out_rows):
    n_blocks = blk_e.shape[0]
    n_tok = h2.shape[0] // ROW_SUB
    assert n_tok & (n_tok - 1) == 0, "token count must be a power of two"
    tm = MOE_TILE
    grid_spec = pltpu.PrefetchScalarGridSpec(
        num_scalar_prefetch=2,
        grid=(n_blocks,),
        in_specs=[pl.BlockSpec(memory_space=pl.ANY),
                  pl.BlockSpec(memory_space=pl.ANY),
                  pl.BlockSpec((1, D_MODEL, 2 * D_FF), lambda i, be, na: (be[i], 0, 0)),
                  pl.BlockSpec((1, 1, 2 * D_FF), lambda i, be, na: (be[i], 0, 0)),
                  pl.BlockSpec((1, D_FF, D_MODEL), lambda i, be, na: (be[i], 0, 0)),
                  pl.BlockSpec((1, 1, D_MODEL), lambda i, be, na: (be[i], 0, 0))],
        out_specs=pl.BlockSpec(memory_space=pl.ANY),
        scratch_shapes=[pltpu.SMEM((IDX_RING, tm), jnp.int32),
                        pltpu.VMEM((2, tm * ROW_SUB, LANES), F32),
                        pltpu.VMEM((2, tm * ROW_SUB, LANES), F32),
                        pltpu.VMEM((tm, D_MODEL), BF16),
                        pltpu.SemaphoreType.DMA((IDX_RING,)),
                        pltpu.SemaphoreType.DMA((2,)),
                        pltpu.SemaphoreType.DMA((2,))],
    )
    return pl.pallas_call(
        functools.partial(_moe_kernel, n_tok=n_tok),
        out_shape=jax.ShapeDtypeStruct((out_rows * ROW_SUB, LANES), F32),
        grid_spec=grid_spec,
        compiler_params=pltpu.CompilerParams(dimension_semantics=("arbitrary",), vmem_limit_bytes=VMEM_LIMIT,
                                             disable_bounds_checks=True),
        name="moe",
    )(blk_e, nact, slot_asg.reshape(n_blocks, tm), h2, wgu_bf, b_gate_up.reshape(N_EXPERTS, 1, -1),
      wd_bf, b_down.reshape(N_EXPERTS, 1, -1))


COMBINE_TILE = 512


def _combine_kernel(o0_ref, o1_ref, o2_ref, o3_ref, x1_ref, w_ref, mod_ref, g_ref, b_ref, o_ref):
    w = w_ref[...]
    parts = []
    for j in range(SUBLANES):
        blk = _lane_block(j, w.shape[0])
        parts.append(w[:, 0:1] * o0_ref[blk] + w[:, 1:2] * o1_ref[blk]
                     + w[:, 2:3] * o2_ref[blk] + w[:, 3:4] * o3_ref[blk])
    y = jnp.concatenate(parts, axis=1)
    m = mod_ref[0]
    o_ref[...] = _ln(DEEPNORM_ALPHA * x1_ref[...] + m[5:6] * y) * g_ref[...] + b_ref[...]


def _combine(outs, x1, top_w, mod3, ln2_g, ln2_b, seq, b0):
    t = x1.shape[0]
    tk = COMBINE_TILE
    nt = t // tk
    full = pl.BlockSpec((tk, D_MODEL), lambda i: (i, 0))
    vec = pl.BlockSpec((1, D_MODEL), lambda i: (0, 0))
    expert_rows = [pl.BlockSpec((tk * ROW_SUB, LANES), lambda i, k=k: (k * nt + i, 0)) for k in range(TOP_K)]
    return pl.pallas_call(
        _combine_kernel,
        out_shape=jax.ShapeDtypeStruct((t, D_MODEL), F32),
        grid=(nt,),
        in_specs=expert_rows + [full, pl.BlockSpec((tk, TOP_K), lambda i: (i, 0)),
                                pl.BlockSpec((1, 6, D_MODEL), lambda i: (b0 + (i * tk) // seq, 0, 0)),
                                vec, vec],
        out_specs=full,
        compiler_params=_params(("arbitrary",)),
        name="combine",
    )(outs, outs, outs, outs, x1, top_w, mod3, ln2_g.reshape(1, -1), ln2_b.reshape(1, -1))


def _gate_weights(w_rg, b_rg):
    hpb = LANES // REC_HEAD_DIM
    nblk = D_REC // LANES
    w = w_rg.reshape(2, 2, nblk, hpb, REC_HEAD_DIM, REC_HEAD_DIM)
    eye = jnp.eye(hpb, dtype=w.dtype)
    blk = jnp.einsum('dgbhij,hk->dgbhikj', w, eye).reshape(2, 2, nblk, LANES, LANES)
    wg = jnp.concatenate([blk[:, 0], blk[:, 1]], axis=-1).astype(BF16)
    bg = b_rg.reshape(2, 2, D_REC)
    return wg, bg


def _routing_tables(top_e, counts, tm):
    t = top_e.shape[0]
    n_asg = t * TOP_K
    assert t & (t - 1) == 0 and TOP_K == 4
    counts = counts.reshape(N_EXPERTS).astype(jnp.int32)
    padded = (counts + tm - 1) // tm * tm
    pad_ends = jnp.cumsum(padded)
    n_blocks = n_asg // tm + N_EXPERTS
    blk_start = jnp.arange(n_blocks, dtype=jnp.int32) * tm
    blk_e = jnp.minimum(jnp.sum((pad_ends[None, :] <= blk_start[:, None]).astype(jnp.int32), axis=1),
                        N_EXPERTS - 1)
    nact = (pad_ends[-1] // tm).astype(jnp.int32).reshape(1)
    tok = jnp.arange(t, dtype=jnp.int32)[None, :]
    kk = jnp.arange(TOP_K, dtype=jnp.int32)[:, None]
    real = (top_e.T * (2 * t) + tok) * TOP_K + kk
    e = jnp.arange(N_EXPERTS, dtype=jnp.int32)[:, None]
    m = jnp.arange(tm, dtype=jnp.int32)[None, :]
    unused = (N_EXPERTS * 2 * t + t) * TOP_K
    pads = jnp.where(m < (padded - counts)[:, None], (e * (2 * t) + t + m) * TOP_K, unused)
    key = jnp.sort(jnp.concatenate([real.reshape(-1), pads.reshape(-1)]))
    field = lax.shift_right_logical(key, 2) & (2 * t - 1)
    slot = jnp.arange(n_blocks * tm, dtype=jnp.int32)
    slot_asg = jnp.where(field >= t, n_asg + (slot & (tm - 1)), (key & (TOP_K - 1)) * t + field)
    return slot_asg, blk_e, nact


def _trunk(x, mod3, b0, p):
    batch, seq, _ = x.shape
    t = batch * seq
    x2d = x.reshape(t, D_MODEL)
    ux, ug, zr, zi = _inproj(x2d, mod3, p['w_in_bf'], seq, b0)
    rec = _rg_lru(ux, ug, p['conv_w'], p['conv_b'], p['wg'], p['bg'], p['lam'], batch, seq)
    four = _seq_dft(zr, zi, batch, seq)
    x1, h2, top_e, top_w, counts = _mix_route(rec, four, x2d, mod3, p['g_mix'], p['w_out_bf'],
                                                    p['ln1_g'], p['ln1_b'], p['w_router'], p['b_router'],
                                                    seq, b0)
    slot_asg, blk_e, nact = _routing_tables(top_e, counts, MOE_TILE)
    outs = _moe(h2, slot_asg, blk_e, nact, p['wgu_bf'], p['b_gate_up'], p['wd_bf'], p['b_down'],
                out_rows=t * TOP_K + MOE_TILE)
    y = _combine(outs, x1, top_w, mod3, p['ln2_g'], p['ln2_b'], seq, b0)
    return y.reshape(batch, seq, D_MODEL)


def kernel(x_prompt, x_sample, c_prompt, c_sample, w_ada, b_ada, w_in, conv_w, conv_b, w_rg, b_rg,
           lru_lambda, g_mix, w_out, ln1_g, ln1_b, w_router, b_router, w_gate_up, b_gate_up,
           w_down, b_down, ln2_g, ln2_b):
    depth = w_ada.shape[0]
    xs = [x_prompt, x_sample]
    cs = [c_prompt, c_sample]
    nb = [c.shape[0] for c in cs]
    bp = -(-sum(nb) // SUBLANES) * SUBLANES
    for l in range(depth):
        wg, bg = _gate_weights(w_rg[l], b_rg[l])
        p = dict(w_in_bf=w_in[l].astype(BF16), conv_w=conv_w[l], conv_b=conv_b[l].reshape(1, -1),
                 wg=wg, bg=bg, lam=lru_lambda[l], g_mix=g_mix[l], w_out_bf=w_out[l].astype(BF16),
                 ln1_g=ln1_g[l], ln1_b=ln1_b[l], w_router=w_router[l], b_router=b_router[l],
                 wgu_bf=w_gate_up[l].astype(BF16), b_gate_up=b_gate_up[l],
                 wd_bf=w_down[l].astype(BF16), b_down=b_down[l], ln2_g=ln2_g[l], ln2_b=ln2_b[l])
        c_all = jnp.concatenate(cs + [jnp.zeros((bp - sum(nb), D_MODEL), F32)], axis=0)
        mod3 = _ada_mod(c_all, w_ada[l], b_ada[l]).reshape(bp, 6, D_MODEL)
        xs = [_trunk(xs[0], mod3, 0, p), _trunk(xs[1], mod3, nb[0], p)]
    return (xs[0], xs[1])
```

```python
import functools
import math

import numpy as np
import jax
import jax.numpy as jnp
from jax import lax
from jax.experimental import pallas as pl
from jax.experimental.pallas import tpu as pltpu

D_MODEL = 1024
D_REC = 512
D_FFT = 512
N_REC_HEADS = 8
REC_HEAD_DIM = 64
FFT_GROUP_DIM = 128
N_FFT_GROUPS = 4
CONV_WIDTH = 4
CONV_LEFT = 2
LRU_C = 8.0
N_EXPERTS = 32
TOP_K = 4
D_FF = 1024
SWIGLU_LIMIT = 7.0
SWIGLU_ALPHA = 1.702
DEEPNORM_ALPHA = 2.0 ** 0.25
LN_EPS = 1e-5

LANES = 128
SUBLANES = 8
ROW_SUB = D_MODEL // LANES
assert ROW_SUB == SUBLANES
DFT_N2 = 128
VMEM_LIMIT = 48 * 1024 * 1024

F32 = jnp.float32
BF16 = jnp.bfloat16


def _lane_block(j, n):
    return (pl.ds(j, n, stride=ROW_SUB), slice(None))


def _params(sem, vmem=VMEM_LIMIT):
    return pltpu.CompilerParams(dimension_semantics=sem, vmem_limit_bytes=vmem)


def _dot(a, b):
    return jnp.dot(a, b, preferred_element_type=F32)


def _dot3(a, b):
    a_hi = a.astype(BF16)
    b_hi = b.astype(BF16)
    a_lo = (a - a_hi.astype(F32)).astype(BF16)
    b_lo = (b - b_hi.astype(F32)).astype(BF16)
    return _dot(a_hi, b_hi) + _dot(a_lo, b_hi) + _dot(a_hi, b_lo)


def _ln(x):
    mu = jnp.mean(x, axis=-1, keepdims=True)
    xc = x - mu
    var = jnp.mean(xc * xc, axis=-1, keepdims=True)
    return xc * lax.rsqrt(var + LN_EPS)


def _ada_kernel(c_ref, w_ref, b_ref, o_ref):
    c = c_ref[...]
    s = c * jax.nn.sigmoid(c)
    o_ref[...] = _dot3(s, w_ref[...]) + b_ref[...]


def _ada_mod(c_all, w_ada, b_ada):
    bp = c_all.shape[0]
    n = w_ada.shape[1]
    tn = 768
    return pl.pallas_call(
        _ada_kernel,
        out_shape=jax.ShapeDtypeStruct((bp, n), F32),
        grid=(n // tn,),
        in_specs=[pl.BlockSpec((bp, D_MODEL), lambda j: (0, 0)),
                  pl.BlockSpec((D_MODEL, tn), lambda j: (0, j)),
                  pl.BlockSpec((1, tn), lambda j: (0, j))],
        out_specs=pl.BlockSpec((bp, tn), lambda j: (0, j)),
        compiler_params=_params(("arbitrary",)),
        name="ada_mod",
    )(c_all, w_ada, b_ada.reshape(1, n))


def _inproj_kernel(x_ref, mod_ref, w_ref, dft_ref, ux_ref, ug_ref, zr_ref, zi_ref):
    m = mod_ref[0]
    h = _ln(x_ref[...]) * (1.0 + m[1:2]) + m[0:1]
    u = _dot(h.astype(BF16), w_ref[...])
    ux_ref[...] = u[:, :D_REC]
    ug_ref[...] = u[:, D_REC:2 * D_REC]
    for g in range(N_FFT_GROUPS):
        lo = 2 * D_REC + g * FFT_GROUP_DIM
        z = _dot(u[:, lo:lo + FFT_GROUP_DIM].astype(BF16), dft_ref[...])
        zr_ref[:, g * FFT_GROUP_DIM:(g + 1) * FFT_GROUP_DIM] = z[:, :FFT_GROUP_DIM]
        zi_ref[:, g * FFT_GROUP_DIM:(g + 1) * FFT_GROUP_DIM] = z[:, FFT_GROUP_DIM:]


def _group_dft_matrix():
    j = np.arange(FFT_GROUP_DIM)
    ang = 2.0 * np.pi * ((j[:, None] * j[None, :]) % FFT_GROUP_DIM) / FFT_GROUP_DIM
    return jnp.asarray(np.concatenate([np.cos(ang), -np.sin(ang)], axis=1), dtype=F32).astype(BF16)


def _inproj(x2d, mod3, w_in_bf, seq, b0):
    t = x2d.shape[0]
    tm = 1024
    row = pl.BlockSpec((tm, D_REC), lambda i: (i, 0))
    return pl.pallas_call(
        _inproj_kernel,
        out_shape=[jax.ShapeDtypeStruct((t, D_REC), F32)] * 4,
        grid=(t // tm,),
        in_specs=[pl.BlockSpec((tm, D_MODEL), lambda i: (i, 0)),
                  pl.BlockSpec((1, 6, D_MODEL), lambda i: (b0 + (i * tm) // seq, 0, 0)),
                  pl.BlockSpec((D_MODEL, 3 * D_REC), lambda i: (0, 0)),
                  pl.BlockSpec((FFT_GROUP_DIM, 2 * FFT_GROUP_DIM), lambda i: (0, 0))],
        out_specs=[row, row, row, row],
        compiler_params=_params(("arbitrary",)),
        name="inproj",
    )(x2d, mod3, w_in_bf, _group_dft_matrix())


REC_CHUNK = 512


def _scan_chunk(a, b, carry, h_ref, reverse):
    rows, width = a.shape
    n_tiles = rows // SUBLANES
    a = a.reshape(n_tiles, SUBLANES, width)
    b = b.reshape(n_tiles, SUBLANES, width)
    ridx = lax.broadcasted_iota(jnp.int32, a.shape, 1)
    for s in (1, 2, 4):
        shift = SUBLANES - s if reverse else s
        keep = ridx < SUBLANES - s if reverse else ridx >= s
        a_sh = pltpu.roll(a, shift, 1)
        b_sh = pltpu.roll(b, shift, 1)
        b = jnp.where(keep, a * b_sh + b, b)
        a = jnp.where(keep, a * a_sh, a)
    order = range(n_tiles - 1, -1, -1) if reverse else range(n_tiles)
    for j in order:
        h = b[j] + a[j] * carry
        carry = h[0:1] if reverse else h[SUBLANES - 1:SUBLANES]
        h_ref[j * SUBLANES:(j + 1) * SUBLANES, :] = h
    return carry


def _conv_chunk(prev_ref, x_ref, next_ref, cw_ref, cb_ref, first, last):
    rows = x_ref.shape[0]
    prev = jnp.where(first, 0.0, prev_ref[...])
    nxt = jnp.where(last, 0.0, next_ref[...])
    ext = jnp.concatenate([prev, x_ref[...], nxt], axis=0)
    n = rows + 2 * SUBLANES
    body = slice(SUBLANES, SUBLANES + rows)
    cw = cw_ref[...]
    xc = cw[2:3] * ext[body]
    xc = xc + cw[0:1] * pltpu.roll(ext, 2, 0)[body]
    xc = xc + cw[1:2] * pltpu.roll(ext, 1, 0)[body]
    xc = xc + cw[3:4] * pltpu.roll(ext, n - 1, 0)[body]
    return xc + cb_ref[...]


def _sigmoid(x):
    return 0.5 * jnp.tanh(0.5 * x) + 0.5


def _neg_expm1_of_double(half_y):
    t = jnp.tanh(half_y)
    return (-2.0 * t) / (1.0 - t)


def _lru_coeffs(xc, wg_ref, bg_ref, lam_ref):
    lam = lam_ref[...]
    neg = -lam
    softplus = jnp.maximum(neg, 0.0) + jnp.log(1.0 + jnp.exp(-jnp.abs(neg)))
    xb = xc.astype(BF16)
    r_parts, i_parts = [], []
    for cb in range(D_REC // LANES):
        g = _dot(xb[:, cb * LANES:(cb + 1) * LANES], wg_ref[cb])
        r_parts.append(g[:, :LANES])
        i_parts.append(g[:, LANES:])
    bg = bg_ref[...]
    r = _sigmoid(jnp.concatenate(r_parts, axis=1) + bg[0:1])
    i = _sigmoid(jnp.concatenate(i_parts, axis=1) + bg[1:2])
    log_a = (-LRU_C) * r * softplus
    a = jnp.exp(log_a)
    mult = jnp.sqrt(_neg_expm1_of_double(log_a))
    return a, mult * (i * xc)


def _rec_fwd_kernel(prev_ref, x_ref, next_ref, cw_ref, cb_ref, wg_ref, bg_ref, lam_ref,
                    hf_ref, carry_ref):
    c = pl.program_id(1)
    nch = pl.num_programs(1)

    @pl.when(c == 0)
    def _():
        carry_ref[...] = jnp.zeros_like(carry_ref)

    xc = _conv_chunk(prev_ref, x_ref, next_ref, cw_ref, cb_ref, c == 0, c == nch - 1)
    a, b = _lru_coeffs(xc, wg_ref, bg_ref, lam_ref)
    carry_ref[...] = _scan_chunk(a, b, carry_ref[...], hf_ref, reverse=False)


def _rec_bwd_kernel(prev_ref, x_ref, next_ref, cw_ref, cb_ref, wg_ref, bg_ref, lam_ref,
                    hf_ref, ug_ref, rec_ref, carry_ref, hb_ref):
    j = pl.program_id(1)
    nch = pl.num_programs(1)

    @pl.when(j == 0)
    def _():
        carry_ref[...] = jnp.zeros_like(carry_ref)

    xc = _conv_chunk(prev_ref, x_ref, next_ref, cw_ref, cb_ref, j == nch - 1, j == 0)
    a, b = _lru_coeffs(xc, wg_ref, bg_ref, lam_ref)
    carry_ref[...] = _scan_chunk(a, b, carry_ref[...], hb_ref, reverse=True)
    ug = ug_ref[...]
    gelu = 0.5 * ug * (1.0 + jnp.tanh(math.sqrt(2.0 / math.pi) * (ug + 0.044715 * (ug * ug * ug))))
    rec_ref[...] = (hf_ref[...] + hb_ref[...]) * gelu


def _rec_specs(batch, seq, reverse):
    lc = min(REC_CHUNK, seq)
    nch = seq // lc
    tiles = lc // SUBLANES
    last_tile = batch * nch * tiles - 1

    def chunk(b, j):
        return nch - 1 - j if reverse else j

    row = pl.BlockSpec((lc, D_REC), lambda b, j: (b * nch + chunk(b, j), 0))
    prev = pl.BlockSpec((SUBLANES, D_REC),
                        lambda b, j: (jnp.maximum((b * nch + chunk(b, j)) * tiles - 1, 0), 0))
    nxt = pl.BlockSpec((SUBLANES, D_REC),
                       lambda b, j: (jnp.minimum((b * nch + chunk(b, j) + 1) * tiles, last_tile), 0))
    const2 = lambda shape: pl.BlockSpec(shape, lambda b, j: (0,) * len(shape))
    weights = [const2((CONV_WIDTH, D_REC)), const2((1, D_REC)),
               const2((D_REC // LANES, LANES, 2 * LANES)), const2((2, D_REC)), const2((1, D_REC))]
    return row, prev, nxt, weights, nch, lc


def _rg_lru(ux, ug, conv_w, conv_b, wg, bg, lam, batch, seq):
    t = ux.shape[0]
    row, prev, nxt, wspecs, nch, lc = _rec_specs(batch, seq, False)
    hf = pl.pallas_call(
        _rec_fwd_kernel,
        out_shape=jax.ShapeDtypeStruct((t, D_REC), F32),
        grid=(batch, nch),
        in_specs=[prev, row, nxt] + wspecs,
        out_specs=row,
        scratch_shapes=[pltpu.VMEM((1, D_REC), F32)],
        compiler_params=_params(("arbitrary", "arbitrary")),
        name="rec_fwd",
    )(ux, ux, ux, conv_w, conv_b, wg[0], bg[0], lam[0:1])
    row, prev, nxt, wspecs, nch, lc = _rec_specs(batch, seq, True)
    return pl.pallas_call(
        _rec_bwd_kernel,
        out_shape=jax.ShapeDtypeStruct((t, D_REC), F32),
        grid=(batch, nch),
        in_specs=[prev, row, nxt] + wspecs + [row, row],
        out_specs=row,
        scratch_shapes=[pltpu.VMEM((1, D_REC), F32), pltpu.VMEM((lc, D_REC), F32)],
        compiler_params=_params(("arbitrary", "arbitrary")),
        name="rec_bwd",
    )(ux, ux, ux, conv_w, conv_b, wg[1], bg[1], lam[1:2], hf, ug)


DFT_A_JB = 8
DFT_C_GB = 8


def _dft_a_kernel(zr_ref, zi_ref, t_ref, a_ref):
    for jb in range(zr_ref.shape[2]):
        for r in range(SUBLANES):
            rhs = jnp.concatenate([zr_ref[0, :, jb, r, :], zi_ref[0, :, jb, r, :]], axis=0)
            a_ref[0, 0, :, jb, r, :] = _dot(t_ref[jb * SUBLANES + r], rhs.astype(BF16))


def _dft_c_kernel(a_ref, m_ref, o_ref, *, scale):
    for g in range(a_ref.shape[3]):
        for j in range(SUBLANES):
            rhs = jnp.concatenate([a_ref[0, 0, 0, g, j], a_ref[0, 0, 1, g, j]], axis=0)
            o_ref[0, :, g, j, :] = _dot(m_ref[...], rhs.astype(BF16)) * scale


def _dft_tables(seq):
    n2 = DFT_N2
    n1 = seq // n2
    k1 = jnp.arange(n1, dtype=jnp.int32)
    s = n2 * jnp.arange(n1, dtype=jnp.int32)[None, None, :] + jnp.arange(n2, dtype=jnp.int32)[:, None, None]
    ang = (2.0 * math.pi / seq) * ((k1[None, :, None] * s) % seq).astype(F32)
    cs, sn = jnp.cos(ang), jnp.sin(ang)
    t_a = jnp.concatenate([jnp.concatenate([cs, sn], axis=2),
                           jnp.concatenate([-sn, cs], axis=2)], axis=1).astype(BF16)
    k2 = jnp.arange(n2, dtype=jnp.int32)
    phi = (2.0 * math.pi / n2) * ((k2[:, None] * k2[None, :]) % n2).astype(F32)
    m_c = jnp.concatenate([jnp.cos(phi), jnp.sin(phi)], axis=1).astype(BF16)
    return t_a, m_c


def _seq_dft(zr, zi, batch, seq):
    n2 = DFT_N2
    n1 = seq // n2
    nj = n2 // SUBLANES
    ncb = D_FFT // LANES
    t_a, m_c = _dft_tables(seq)
    z5 = (batch, n1, nj, SUBLANES, D_FFT)
    jb = min(DFT_A_JB, nj)
    zspec = pl.BlockSpec((1, n1, jb, SUBLANES, LANES), lambda b, cb, j: (b, 0, j, 0, cb))
    a6 = pl.pallas_call(
        _dft_a_kernel,
        out_shape=jax.ShapeDtypeStruct((batch, ncb, 2 * n1, nj, SUBLANES, LANES), F32),
        grid=(batch, ncb, nj // jb),
        in_specs=[zspec, zspec,
                  pl.BlockSpec((jb * SUBLANES, 2 * n1, 2 * n1), lambda b, cb, j: (j, 0, 0))],
        out_specs=pl.BlockSpec((1, 1, 2 * n1, jb, SUBLANES, LANES), lambda b, cb, j: (b, cb, 0, j, 0, 0)),
        compiler_params=_params(("arbitrary",) * 3),
        name="dft_a",
    )(zr.reshape(z5), zi.reshape(z5), t_a)
    ng = n1 // SUBLANES
    gb = min(DFT_C_GB, ng)
    a6 = a6.reshape(batch, ncb, 2, ng, SUBLANES, n2, LANES)
    out = pl.pallas_call(
        functools.partial(_dft_c_kernel, scale=1.0 / math.sqrt(seq * FFT_GROUP_DIM)),
        out_shape=jax.ShapeDtypeStruct((batch, n2, ng, SUBLANES, D_FFT), F32),
        grid=(batch, ncb, ng // gb),
        in_specs=[pl.BlockSpec((1, 1, 2, gb, SUBLANES, n2, LANES), lambda b, cb, g: (b, cb, 0, g, 0, 0, 0)),
                  pl.BlockSpec((n2, 2 * n2), lambda b, cb, g: (0, 0))],
        out_specs=pl.BlockSpec((1, n2, gb, SUBLANES, LANES), lambda b, cb, g: (b, 0, g, 0, cb)),
        compiler_params=_params(("arbitrary",) * 3),
        name="dft_c",
    )(a6, m_c)
    return out.reshape(batch * seq, D_FFT)


ROUTE_TILE = 512


def _mix_route_kernel(rec_ref, four_ref, x_ref, mod_ref, gmix_ref, wout_ref, ln1g_ref, ln1b_ref,
                      wr_ref, br_ref,
                      x1_ref, h2_ref, tope_ref, topw_ref, cnt_ref, run_ref):
    i = pl.program_id(0)

    @pl.when(i == 0)
    def _():
        run_ref[...] = jnp.zeros_like(run_ref)

    m = mod_ref[0]
    gm = gmix_ref[...]
    rec = rec_ref[...]
    four = four_ref[...]
    rn = rec * lax.rsqrt(jnp.mean(rec * rec, axis=-1, keepdims=True) + LN_EPS) * gm[:, :D_REC]
    fn = four * lax.rsqrt(jnp.mean(four * four, axis=-1, keepdims=True) + LN_EPS) * gm[:, D_REC:]
    mix = _dot(rn.astype(BF16), wout_ref[:D_REC, :]) + _dot(fn.astype(BF16), wout_ref[D_REC:, :])
    x1 = _ln(DEEPNORM_ALPHA * x_ref[...] + m[2:3] * mix) * ln1g_ref[...] + ln1b_ref[...]
    x1_ref[...] = x1
    h2 = _ln(x1) * (1.0 + m[4:5]) + m[3:4]
    for j in range(SUBLANES):
        h2_ref[_lane_block(j, h2.shape[0])] = h2[:, j * LANES:(j + 1) * LANES]
    logits = _dot3(h2, wr_ref[...]) + br_ref[...]
    lane = lax.broadcasted_iota(jnp.int32, logits.shape, 1).astype(F32)
    work = logits
    onehot = jnp.zeros(logits.shape, F32)
    vals, idxs = [], []
    for _ in range(TOP_K):
        v = jnp.max(work, axis=-1, keepdims=True)
        e = jnp.min(jnp.where(work == v, lane, float(N_EXPERTS)), axis=-1, keepdims=True)
        sel = lane == e
        onehot = jnp.where(sel, 1.0, onehot)
        work = jnp.where(sel, -jnp.inf, work)
        vals.append(v)
        idxs.append(e)
    ex = [jnp.exp(v - vals[0]) for v in vals]
    denom = ex[0] + ex[1] + ex[2] + ex[3]
    for k in range(TOP_K):
        tope_ref[:, k:k + 1] = idxs[k].astype(jnp.int32)
        topw_ref[:, k:k + 1] = ex[k] / denom
    run_ref[...] = run_ref[...] + jnp.sum(onehot, axis=0, keepdims=True)
    cnt_ref[...] = run_ref[...]


def _mix_route(rec, four, x2d, mod3, g_mix, w_out_bf, ln1_g, ln1_b, w_router, b_router, seq, b0):
    t = x2d.shape[0]
    tm = ROUTE_TILE
    half = pl.BlockSpec((tm, D_REC), lambda i: (i, 0))
    full = pl.BlockSpec((tm, D_MODEL), lambda i: (i, 0))
    vec = pl.BlockSpec((1, D_MODEL), lambda i: (0, 0))
    k4 = pl.BlockSpec((tm, TOP_K), lambda i: (i, 0))
    return pl.pallas_call(
        _mix_route_kernel,
        out_shape=[jax.ShapeDtypeStruct((t, D_MODEL), F32), jax.ShapeDtypeStruct((t * ROW_SUB, LANES), F32),
                   jax.ShapeDtypeStruct((t, TOP_K), jnp.int32), jax.ShapeDtypeStruct((t, TOP_K), F32),
                   jax.ShapeDtypeStruct((1, N_EXPERTS), F32)],
        grid=(t // tm,),
        in_specs=[half, half, full,
                  pl.BlockSpec((1, 6, D_MODEL), lambda i: (b0 + (i * tm) // seq, 0, 0)),
                  vec, pl.BlockSpec((D_MODEL, D_MODEL), lambda i: (0, 0)), vec, vec,
                  pl.BlockSpec((D_MODEL, N_EXPERTS), lambda i: (0, 0)),
                  pl.BlockSpec((1, N_EXPERTS), lambda i: (0, 0))],
        out_specs=[full, pl.BlockSpec((tm * ROW_SUB, LANES), lambda i: (i, 0)), k4, k4,
                   pl.BlockSpec((1, N_EXPERTS), lambda i: (0, 0))],
        scratch_shapes=[pltpu.VMEM((1, N_EXPERTS), F32)],
        compiler_params=_params(("arbitrary",)),
        name="mix_route",
    )(rec, four, x2d, mod3, g_mix.reshape(1, -1), w_out_bf, ln1_g.reshape(1, -1), ln1_b.reshape(1, -1),
      w_router, b_router.reshape(1, -1))


MOE_TILE = 512
IDX_RING = 4


def _moe_kernel(blk_e_ref, nact_ref, asg_hbm, h_hbm, wgu_ref, bgu_ref, wd_ref, bd_ref, out_hbm,
                idx_ref, xbuf_ref, obuf_ref, xb_ref, idx_sem, row_sem, sc_sem, *, n_tok):
    i = pl.program_id(0)
    nact = nact_ref[0]
    tm = MOE_TILE
    last = nact - 1

    def idx_copy(blk, ring):
        return pltpu.make_async_copy(asg_hbm.at[blk], idx_ref.at[ring], idx_sem.at[ring])

    def gather_rows(ring, slot):
        for r in range(tm):
            src = pl.multiple_of((idx_ref[ring, r] & (n_tok - 1)) * ROW_SUB, ROW_SUB)
            pltpu.make_async_copy(h_hbm.at[pl.ds(src, ROW_SUB)], xbuf_ref.at[slot, pl.ds(r * ROW_SUB, ROW_SUB)],
                                  row_sem.at[slot]).start(priority=r % 2)

    def scatter_rows(ring, slot):
        for r in range(tm):
            row = pl.multiple_of(idx_ref[ring, r] * ROW_SUB, ROW_SUB)
            pltpu.make_async_copy(obuf_ref.at[slot, pl.ds(r * ROW_SUB, ROW_SUB)], out_hbm.at[pl.ds(row, ROW_SUB)],
                                  sc_sem.at[slot]).start(priority=r % 2)

    def wait_gather(slot):
        pltpu.make_async_copy(h_hbm.at[pl.ds(0, tm * ROW_SUB)], xbuf_ref.at[slot], row_sem.at[slot]).wait()

    def wait_scatter(slot):
        pltpu.make_async_copy(obuf_ref.at[slot], out_hbm.at[pl.ds(0, tm * ROW_SUB)], sc_sem.at[slot]).wait()

    def step(with_scatter):
        slot = i % 2
        wait_gather(slot)
        idx_copy(0, (i + 1) % IDX_RING).wait()
        idx_copy(jnp.minimum(i + 2, last), (i + 2) % IDX_RING).start()
        if with_scatter:
            @pl.when(i >= 2)
            def _():
                wait_scatter(slot)
        for j in range(SUBLANES):
            xb_ref[:, j * LANES:(j + 1) * LANES] = xbuf_ref[(slot,) + _lane_block(j, tm)].astype(BF16)
        gather_rows((i + 1) % IDX_RING, 1 - slot)
        if with_scatter:
            scatter_rows((i - 1) % IDX_RING, 1 - slot)
        gu = _dot(xb_ref[...], wgu_ref[0]) + bgu_ref[0]
        gate = jnp.minimum(gu[:, :D_FF], SWIGLU_LIMIT)
        up = jnp.clip(gu[:, D_FF:], -SWIGLU_LIMIT, SWIGLU_LIMIT)
        act = (up + 1.0) * (gate * jax.nn.sigmoid(SWIGLU_ALPHA * gate))
        out = _dot(act.astype(BF16), wd_ref[0]) + bd_ref[0]
        for j in range(SUBLANES):
            obuf_ref[(slot,) + _lane_block(j, tm)] = out[:, j * LANES:(j + 1) * LANES]

    @pl.when(i == 0)
    def _():
        obuf_ref[0] = jnp.zeros(obuf_ref.shape[1:], F32)
        spare = pltpu.make_async_copy(obuf_ref.at[0], out_hbm.at[pl.ds(TOP_K * n_tok * ROW_SUB, tm * ROW_SUB)],
                                      sc_sem.at[0])
        spare.start()
        spare.wait()
        cp = idx_copy(0, 0)
        cp.start()
        cp.wait()
        gather_rows(0, 0)
        idx_copy(jnp.minimum(1, last), 1).start()
        step(False)

    @pl.when(jnp.logical_and(i > 0, i < nact))
    def _():
        step(True)

    @pl.when(i == nact)
    def _():
        wait_gather(i % 2)
        idx_copy(0, (i + 1) % IDX_RING).wait()

        @pl.when(i >= 2)
        def _():
            wait_scatter(i % 2)

        scatter_rows((i - 1) % IDX_RING, (i - 1) % 2)
        wait_scatter((i - 1) % 2)


def _moe(h2, slot_asg, blk_e, nact, wgu_bf, b_gate_up, wd_bf, b_down, out_rows):
    n_blocks = blk_e.shape[0]
    n_tok = h2.shape[0] // ROW_SUB
    assert n_tok & (n_tok - 1) == 0, "token count must be a power of two"
    tm = MOE_TILE
    grid_spec = pltpu.PrefetchScalarGridSpec(
        num_scalar_prefetch=2,
        grid=(n_blocks,),
        in_specs=[pl.BlockSpec(memory_space=pl.ANY),
                  pl.BlockSpec(memory_space=pl.ANY),
                  pl.BlockSpec((1, D_MODEL, 2 * D_FF), lambda i, be, na: (be[i], 0, 0)),
                  pl.BlockSpec((1, 1, 2 * D_FF), lambda i, be, na: (be[i], 0, 0)),
                  pl.BlockSpec((1, D_FF, D_MODEL), lambda i, be, na: (be[i], 0, 0)),
                  pl.BlockSpec((1, 1, D_MODEL), lambda i, be, na: (be[i], 0, 0))],
        out_specs=pl.BlockSpec(memory_space=pl.ANY),
        scratch_shapes=[pltpu.SMEM((IDX_RING, tm), jnp.int32),
                        pltpu.VMEM((2, tm * ROW_SUB, LANES), F32),
                        pltpu.VMEM((2, tm * ROW_SUB, LANES), F32),
                        pltpu.VMEM((tm, D_MODEL), BF16),
                        pltpu.SemaphoreType.DMA((IDX_RING,)),
                        pltpu.SemaphoreType.DMA((2,)),
                        pltpu.SemaphoreType.DMA((2,))],
    )
    return pl.pallas_call(
        functools.partial(_moe_kernel, n_tok=n_tok),
        out_shape=jax.ShapeDtypeStruct((out_rows * ROW_SUB, LANES), F32),
        grid_spec=grid_spec,
        compiler_params=pltpu.CompilerParams(dimension_semantics=("arbitrary",), vmem_limit_bytes=VMEM_LIMIT,
                                             disable_bounds_checks=True),
        name="moe",
    )(blk_e, nact, slot_asg.reshape(n_blocks, tm), h2, wgu_bf, b_gate_up.reshape(N_EXPERTS, 1, -1),
      wd_bf, b_down.reshape(N_EXPERTS, 1, -1))


COMBINE_TILE = 512


def _combine_kernel(o0_ref, o1_ref, o2_ref, o3_ref, x1_ref, w_ref, mod_ref, g_ref, b_ref, o_ref):
    w = w_ref[...]
    parts = []
    for j in range(SUBLANES):
        blk = _lane_block(j, w.shape[0])
        parts.append(w[:, 0:1] * o0_ref[blk] + w[:, 1:2] * o1_ref[blk]
                     + w[:, 2:3] * o2_ref[blk] + w[:, 3:4] * o3_ref[blk])
    y = jnp.concatenate(parts, axis=1)
    m = mod_ref[0]
    o_ref[...] = _ln(DEEPNORM_ALPHA * x1_ref[...] + m[5:6] * y) * g_ref[...] + b_ref[...]


def _combine(outs, x1, top_w, mod3, ln2_g, ln2_b, seq, b0):
    t = x1.shape[0]
    tk = COMBINE_TILE
    nt = t // tk
    full = pl.BlockSpec((tk, D_MODEL), lambda i: (i, 0))
    vec = pl.BlockSpec((1, D_MODEL), lambda i: (0, 0))
    expert_rows = [pl.BlockSpec((tk * ROW_SUB, LANES), lambda i, k=k: (k * nt + i, 0)) for k in range(TOP_K)]
    return pl.pallas_call(
        _combine_kernel,
        out_shape=jax.ShapeDtypeStruct((t, D_MODEL), F32),
        grid=(nt,),
        in_specs=expert_rows + [full, pl.BlockSpec((tk, TOP_K), lambda i: (i, 0)),
                                pl.BlockSpec((1, 6, D_MODEL), lambda i: (b0 + (i * tk) // seq, 0, 0)),
                                vec, vec],
        out_specs=full,
        compiler_params=_params(("arbitrary",)),
        name="combine",
    )(outs, outs, outs, outs, x1, top_w, mod3, ln2_g.reshape(1, -1), ln2_b.reshape(1, -1))


def _gate_weights(w_rg, b_rg):
    hpb = LANES // REC_HEAD_DIM
    nblk = D_REC // LANES
    w = w_rg.reshape(2, 2, nblk, hpb, REC_HEAD_DIM, REC_HEAD_DIM)
    eye = jnp.eye(hpb, dtype=w.dtype)
    blk = jnp.einsum('dgbhij,hk->dgbhikj', w, eye).reshape(2, 2, nblk, LANES, LANES)
    wg = jnp.concatenate([blk[:, 0], blk[:, 1]], axis=-1).astype(BF16)
    bg = b_rg.reshape(2, 2, D_REC)
    return wg, bg


def _routing_tables(top_e, counts, tm):
    t = top_e.shape[0]
    n_asg = t * TOP_K
    assert t & (t - 1) == 0 and TOP_K == 4
    counts = counts.reshape(N_EXPERTS).astype(jnp.int32)
    padded = (counts + tm - 1) // tm * tm
    pad_ends = jnp.cumsum(padded)
    n_blocks = n_asg // tm + N_EXPERTS
    blk_start = jnp.arange(n_blocks, dtype=jnp.int32) * tm
    blk_e = jnp.minimum(jnp.sum((pad_ends[None, :] <= blk_start[:, None]).astype(jnp.int32), axis=1),
                        N_EXPERTS - 1)
    nact = (pad_ends[-1] // tm).astype(jnp.int32).reshape(1)
    tok = jnp.arange(t, dtype=jnp.int32)[None, :]
    kk = jnp.arange(TOP_K, dtype=jnp.int32)[:, None]
    real = (top_e.T * (2 * t) + tok) * TOP_K + kk
    e = jnp.arange(N_EXPERTS, dtype=jnp.int32)[:, None]
    m = jnp.arange(tm, dtype=jnp.int32)[None, :]
    unused = (N_EXPERTS * 2 * t + t) * TOP_K
    pads = jnp.where(m < (padded - counts)[:, None], (e * (2 * t) + t + m) * TOP_K, unused)
    key = jnp.sort(jnp.concatenate([real.reshape(-1), pads.reshape(-1)]))
    field = lax.shift_right_logical(key, 2) & (2 * t - 1)
    slot = jnp.arange(n_blocks * tm, dtype=jnp.int32)
    slot_asg = jnp.where(field >= t, n_asg + (slot & (tm - 1)), (key & (TOP_K - 1)) * t + field)
    return slot_asg, blk_e, nact


def _trunk(x, mod3, b0, p):
    batch, seq, _ = x.shape
    t = batch * seq
    x2d = x.reshape(t, D_MODEL)
    ux, ug, zr, zi = _inproj(x2d, mod3, p['w_in_bf'], seq, b0)
    rec = _rg_lru(ux, ug, p['conv_w'], p['conv_b'], p['wg'], p['bg'], p['lam'], batch, seq)
    four = _seq_dft(zr, zi, batch, seq)
    x1, h2, top_e, top_w, counts = _mix_route(rec, four, x2d, mod3, p['g_mix'], p['w_out_bf'],
                                                    p['ln1_g'], p['ln1_b'], p['w_router'], p['b_router'],
                                                    seq, b0)
    slot_asg, blk_e, nact = _routing_tables(top_e, counts, MOE_TILE)
    outs = _moe(h2, slot_asg, blk_e, nact, p['wgu_bf'], p['b_gate_up'], p['wd_bf'], p['b_down'],
                out_rows=t * TOP_K + MOE_TILE)
    y = _combine(outs, x1, top_w, mod3, p['ln2_g'], p['ln2_b'], seq, b0)
    return y.reshape(batch, seq, D_MODEL)


def kernel(x_prompt, x_sample, c_prompt, c_sample, w_ada, b_ada, w_in, conv_w, conv_b, w_rg, b_rg,
           lru_lambda, g_mix, w_out, ln1_g, ln1_b, w_router, b_router, w_gate_up, b_gate_up,
           w_down, b_down, ln2_g, ln2_b):
    depth = w_ada.shape[0]
    xs = [x_prompt, x_sample]
    cs = [c_prompt, c_sample]
    nb = [c.shape[0] for c in cs]
    bp = -(-sum(nb) // SUBLANES) * SUBLANES
    for l in range(depth):
        wg, bg = _gate_weights(w_rg[l], b_rg[l])
        p = dict(w_in_bf=w_in[l].astype(BF16), conv_w=conv_w[l], conv_b=conv_b[l].reshape(1, -1),
                 wg=wg, bg=bg, lam=lru_lambda[l], g_mix=g_mix[l], w_out_bf=w_out[l].astype(BF16),
                 ln1_g=ln1_g[l], ln1_b=ln1_b[l], w_router=w_router[l], b_router=b_router[l],
                 wgu_bf=w_gate_up[l].astype(BF16), b_gate_up=b_gate_up[l],
                 wd_bf=w_down[l].astype(BF16), b_down=b_down[l], ln2_g=ln2_g[l], ln2_b=ln2_b[l])
        c_all = jnp.concatenate(cs + [jnp.zeros((bp - sum(nb), D_MODEL), F32)], axis=0)
        mod3 = _ada_mod(c_all, w_ada[l], b_ada[l]).reshape(bp, 6, D_MODEL)
        xs = [_trunk(xs[0], mod3, 0, p), _trunk(xs[1], mod3, nb[0], p)]
    return (xs[0], xs[1])
```
